```python
import math
import jax, jax.numpy as jnp
from jax import lax
import numpy as np

D_MODEL = 2048
BATCH = 4
SEQ = 2048
DEPTH = 1
DEC_BATCH = 128
DEC_SEQ = 8
PAST_LEN = 16384
PAGE_SIZE = 128

HEAD_DIM = 128
SB_HEADS = (D_MODEL // 2) // HEAD_DIM
SB_KV_HEADS = 2
SB_GROUP = SB_HEADS // SB_KV_HEADS
MLA_HEADS = (D_MODEL // 2) // HEAD_DIM
MLA_NOPE = 128
MLA_ROPE = 64
MLA_V = 128
Q_LORA = (3 * D_MODEL) // 8
KV_LORA = 512
MEM_TOKENS = 256
MEM_HEADS = 4
MEM_HEAD_DIM = D_MODEL // MEM_HEADS
D_FF = 4 * D_MODEL
Q_BLOCK = 128
ROPE_BASE = 10000.0
EPS = 1e-6
NEG_INF = -1e30
POOL_NUM = 5
POOL_DEN = 4
SB_SCALE = HEAD_DIM ** -0.5
MLA_SCALE = (MLA_NOPE + MLA_ROPE) ** -0.5
MEM_SCALE = MEM_HEAD_DIM ** -0.5

SB_Q_W = SB_HEADS * HEAD_DIM
SB_KV_W = SB_KV_HEADS * HEAD_DIM
IN_SPLITS = [SB_Q_W, SB_Q_W + SB_KV_W, SB_Q_W + 2 * SB_KV_W,
             SB_Q_W + 2 * SB_KV_W + Q_LORA, SB_Q_W + 2 * SB_KV_W + Q_LORA + KV_LORA]
N_IN = SB_Q_W + 2 * SB_KV_W + Q_LORA + KV_LORA + MLA_ROPE
MIX_OUT = SB_HEADS * HEAD_DIM + MLA_HEADS * MLA_V

kernel_name = "sandwich_hybrid_stickbreak_mla_memory_decoder_step"


def rms_norm(x, g):
    x32 = x.astype(jnp.float32)
    y = x32 * lax.rsqrt(jnp.mean(x32 * x32, axis=-1, keepdims=True) + EPS)
    return (y * g.astype(jnp.float32)).astype(x.dtype)


def rope(x, pos):
    half = x.shape[-1] // 2
    inv = ROPE_BASE ** (-jnp.arange(half, dtype=jnp.float32) / half)
    ang = pos.astype(jnp.float32)[:, None] * inv[None, :]
    cos = jnp.cos(ang)[:, None, :]
    sin = jnp.sin(ang)[:, None, :]
    x32 = x.astype(jnp.float32)
    x1, x2 = x32[..., :half], x32[..., half:]
    return jnp.concatenate([x1 * cos - x2 * sin, x2 * cos + x1 * sin], axis=-1).astype(x.dtype)


def project_mixer(h, pos, w_in, q_a_norm, w_q_up, kv_a_norm, w_uk):
    b, t, _ = h.shape
    sb_q, sb_k, sb_v, q_c, c_kv, k_pe = jnp.split(h @ w_in, IN_SPLITS, axis=-1)
    sb_q = sb_q.reshape(b, t, SB_KV_HEADS, SB_GROUP, HEAD_DIM)
    sb_k = sb_k.reshape(b, t, SB_KV_HEADS, HEAD_DIM)
    sb_v = sb_v.reshape(b, t, SB_KV_HEADS, HEAD_DIM)
    q = (rms_norm(q_c, q_a_norm) @ w_q_up).reshape(b, t, MLA_HEADS, MLA_NOPE + MLA_ROPE)
    q_lat = jnp.einsum("bthd,chd->bthc", q[..., :MLA_NOPE], w_uk)
    q_pe = rope(q[..., MLA_NOPE:], pos)
    c_kv = rms_norm(c_kv, kv_a_norm)
    k_pe = rope(k_pe[:, :, None, :], pos)[:, :, 0, :]
    return sb_q, sb_k, sb_v, q_lat, q_pe, c_kv, k_pe


def stick_breaking(q, k, v, q_pos, k_pos):
    z = jnp.einsum("bqngd,bknd->bngqk", q, k, preferred_element_type=jnp.float32) * SB_SCALE
    mask = k_pos[None, :] < q_pos[:, None]
    log_beta = jax.nn.log_sigmoid(z)
    log_one_minus = jnp.where(mask, log_beta - z, 0.0)
    later = lax.cumsum(log_one_minus, axis=z.ndim - 1, reverse=True) - log_one_minus
    a = jnp.where(mask, jnp.exp(log_beta + later), 0.0)
    return jnp.einsum("bngqk,bknd->bqngd", a.astype(v.dtype), v)


def mla_attend(q_lat, q_pe, c_kv, k_pe, q_pos, k_pos):
    s = (jnp.einsum("bqhc,bkc->bhqk", q_lat, c_kv, preferred_element_type=jnp.float32)
         + jnp.einsum("bqhr,bkr->bhqk", q_pe, k_pe, preferred_element_type=jnp.float32)) * MLA_SCALE
    s = jnp.where(k_pos[None, :] <= q_pos[:, None], s, NEG_INF)
    p = jax.nn.softmax(s, axis=-1)
    return jnp.einsum("bhqk,bkc->bqhc", p.astype(c_kv.dtype), c_kv)


def causal_blocks(attend, q_args, kv_args, pos):
    n = pos.shape[0]
    outs = []
    for start in range(0, n, Q_BLOCK):
        end = min(start + Q_BLOCK, n)
        outs.append(attend(*[a[:, start:end] for a in q_args], *[a[:, :end] for a in kv_args],
                           pos[start:end], pos[:end]))
    return jnp.concatenate(outs, axis=1)


def mixer_out(sb_o, mla_lat, w_uv, w_out):
    b, t = sb_o.shape[:2]
    mla_o = jnp.einsum("bthc,chd->bthd", mla_lat, w_uv)
    return jnp.concatenate([sb_o.reshape(b, t, -1), mla_o.reshape(b, t, -1)], axis=-1) @ w_out


def memory_kv(mem, g, w_k, w_v):
    b, m, _ = mem.shape
    mn = rms_norm(mem, g)
    return ((mn @ w_k).reshape(b, m, MEM_HEADS, MEM_HEAD_DIM),
            (mn @ w_v).reshape(b, m, MEM_HEADS, MEM_HEAD_DIM))


def memory_attend(h, mem_k, mem_v, w_q, w_o):
    b, t, _ = h.shape
    q = (h @ w_q).reshape(b, t, MEM_HEADS, MEM_HEAD_DIM)
    s = jnp.einsum("bthd,bmhd->bhtm", q, mem_k, preferred_element_type=jnp.float32) * MEM_SCALE
    p = jax.nn.softmax(s, axis=-1)
    o = jnp.einsum("bhtm,bmhd->bthd", p.astype(mem_v.dtype), mem_v)
    return o.reshape(b, t, -1) @ w_o


def squared_relu_mlp(h, w_up, w_down):
    return jnp.square(jax.nn.relu(h @ w_up)) @ w_down


def gather_pages(cache, page_table):
    g = cache[page_table]
    return g.reshape(page_table.shape[0], -1, *cache.shape[2:])


def setup_inputs(seed: int = 0) -> dict:
    key = jax.random.key(seed)
    ks = iter(jax.random.split(key, 48))
    f32 = jnp.float32
    n_pages = PAST_LEN // PAGE_SIZE
    n_used = DEC_BATCH * n_pages
    n_pool = (POOL_NUM * n_used) // POOL_DEN

    def normal(shape, scale=1.0):
        return jax.random.normal(next(ks), shape, f32) * scale

    def dense(fan_in, fan_out):
        return normal((DEPTH, fan_in, fan_out), fan_in ** -0.5)

    def gain(n):
        return 1.0 + normal((DEPTH, n), 0.05)

    perm = jax.random.permutation(next(ks), n_pool)
    page_table = perm[:n_used].reshape(DEC_BATCH, n_pages).astype(jnp.int32)

    return {
        "x_prompt": normal((BATCH, SEQ, D_MODEL)),
        "x_sample": normal((DEC_BATCH, DEC_SEQ, D_MODEL)),
        "mem_prompt": normal((BATCH, MEM_TOKENS, D_MODEL)),
        "cache_sb_k": normal((DEPTH, n_pool, PAGE_SIZE, SB_KV_HEADS, HEAD_DIM)),
        "cache_sb_v": normal((DEPTH, n_pool, PAGE_SIZE, SB_KV_HEADS, HEAD_DIM)),
        "cache_mla_ckv": normal((DEPTH, n_pool, PAGE_SIZE, KV_LORA)),
        "cache_mla_kpe": normal((DEPTH, n_pool, PAGE_SIZE, MLA_ROPE)),
        "cache_mem_k": normal((DEPTH, DEC_BATCH, MEM_TOKENS, MEM_HEADS, MEM_HEAD_DIM)),
        "cache_mem_v": normal((DEPTH, DEC_BATCH, MEM_TOKENS, MEM_HEADS, MEM_HEAD_DIM)),
        "page_table": page_table,
        "norm_mix_pre": gain(D_MODEL),
        "norm_mix_post": gain(D_MODEL),
        "w_in": dense(D_MODEL, N_IN),
        "q_a_norm": gain(Q_LORA),
        "w_q_up": dense(Q_LORA, MLA_HEADS * (MLA_NOPE + MLA_ROPE)),
        "kv_a_norm": gain(KV_LORA),
        "w_kv_up": dense(KV_LORA, MLA_HEADS * (MLA_NOPE + MLA_V)),
        "w_out": dense(MIX_OUT, D_MODEL),
        "norm_mem": gain(D_MODEL),
        "norm_x_pre": gain(D_MODEL),
        "norm_x_post": gain(D_MODEL),
        "w_xq": dense(D_MODEL, MEM_HEADS * MEM_HEAD_DIM),
        "w_xk": dense(D_MODEL, MEM_HEADS * MEM_HEAD_DIM),
        "w_xv": dense(D_MODEL, MEM_HEADS * MEM_HEAD_DIM),
        "w_xo": dense(MEM_HEADS * MEM_HEAD_DIM, D_MODEL),
        "norm_ffn_pre": gain(D_MODEL),
        "norm_ffn_post": gain(D_MODEL),
        "w_up": dense(D_MODEL, D_FF),
        "w_down": dense(D_FF, D_MODEL),
    }


def reference(x_prompt, x_sample, mem_prompt, cache_sb_k, cache_sb_v, cache_mla_ckv, cache_mla_kpe,
              cache_mem_k, cache_mem_v, page_table, norm_mix_pre, norm_mix_post, w_in, q_a_norm,
              w_q_up, kv_a_norm, w_kv_up, w_out, norm_mem, norm_x_pre, norm_x_post, w_xq, w_xk,
              w_xv, w_xo, norm_ffn_pre, norm_ffn_post, w_up, w_down):
    past_len = page_table.shape[1] * cache_sb_k.shape[2]
    pos_p = jnp.arange(x_prompt.shape[1], dtype=jnp.int32)
    pos_s = past_len + jnp.arange(x_sample.shape[1], dtype=jnp.int32)
    k_pos_s = jnp.arange(past_len + x_sample.shape[1], dtype=jnp.int32)

    xp, xs = x_prompt, x_sample
    sbk_p, sbv_p, ckv_p, kpe_p, mk_p, mv_p = [], [], [], [], [], []
    sbk_s, sbv_s, ckv_s, kpe_s = [], [], [], []
    for l in range(DEPTH):
        w_kv = w_kv_up[l].reshape(KV_LORA, MLA_HEADS, MLA_NOPE + MLA_V)
        w_uk, w_uv = w_kv[..., :MLA_NOPE], w_kv[..., MLA_NOPE:]

        h = rms_norm(xp, norm_mix_pre[l])
        sq, sk, sv, ql, qp, ckv, kpe = project_mixer(h, pos_p, w_in[l], q_a_norm[l], w_q_up[l],
                                                     kv_a_norm[l], w_uk)
        sb_o = causal_blocks(stick_breaking, (sq,), (sk, sv), pos_p)
        mla_lat = causal_blocks(mla_attend, (ql, qp), (ckv, kpe), pos_p)
        xp = xp + rms_norm(mixer_out(sb_o, mla_lat, w_uv, w_out[l]), norm_mix_post[l])
        mk, mv = memory_kv(mem_prompt, norm_mem[l], w_xk[l], w_xv[l])
        xp = xp + rms_norm(memory_attend(rms_norm(xp, norm_x_pre[l]), mk, mv, w_xq[l], w_xo[l]),
                           norm_x_post[l])
        xp = xp + rms_norm(squared_relu_mlp(rms_norm(xp, norm_ffn_pre[l]), w_up[l], w_down[l]),
                           norm_ffn_post[l])
        sbk_p.append(sk); sbv_p.append(sv); ckv_p.append(ckv); kpe_p.append(kpe)
        mk_p.append(mk); mv_p.append(mv)

        h = rms_norm(xs, norm_mix_pre[l])
        sq, sk, sv, ql, qp, ckv, kpe = project_mixer(h, pos_s, w_in[l], q_a_norm[l], w_q_up[l],
                                                     kv_a_norm[l], w_uk)
        k_all = jnp.concatenate([gather_pages(cache_sb_k[l], page_table), sk], axis=1)
        v_all = jnp.concatenate([gather_pages(cache_sb_v[l], page_table), sv], axis=1)
        ckv_all = jnp.concatenate([gather_pages(cache_mla_ckv[l], page_table), ckv], axis=1)
        kpe_all = jnp.concatenate([gather_pages(cache_mla_kpe[l], page_table), kpe], axis=1)
        sb_o = stick_breaking(sq, k_all, v_all, pos_s, k_pos_s)
        mla_lat = mla_attend(ql, qp, ckv_all, kpe_all, pos_s, k_pos_s)
        xs = xs + rms_norm(mixer_out(sb_o, mla_lat, w_uv, w_out[l]), norm_mix_post[l])
        xs = xs + rms_norm(memory_attend(rms_norm(xs, norm_x_pre[l]), cache_mem_k[l], cache_mem_v[l],
                                         w_xq[l], w_xo[l]), norm_x_post[l])
        xs = xs + rms_norm(squared_relu_mlp(rms_norm(xs, norm_ffn_pre[l]), w_up[l], w_down[l]),
                           norm_ffn_post[l])
        sbk_s.append(sk); sbv_s.append(sv); ckv_s.append(ckv); kpe_s.append(kpe)

    return (xp, xs,
            jnp.stack(sbk_p), jnp.stack(sbv_p), jnp.stack(ckv_p), jnp.stack(kpe_p),
            jnp.stack(mk_p), jnp.stack(mv_p),
            jnp.stack(sbk_s), jnp.stack(sbv_s), jnp.stack(ckv_s), jnp.stack(kpe_s))
```

```python
import functools

import jax
import jax.numpy as jnp
from jax import lax
from jax.experimental import pallas as pl
from jax.experimental.pallas import tpu as pltpu

F32 = jnp.float32
BF16 = jnp.bfloat16

EPS = 1e-6
NEG_INF = -1e30
ROPE_BASE = 10000.0
HEAD_DIM = 128
SB_KV_HEADS = 2
SB_GROUP = 4
SB_Q_W = SB_KV_HEADS * SB_GROUP * HEAD_DIM
SB_KV_W = SB_KV_HEADS * HEAD_DIM
MLA_HEADS = 8
MLA_NOPE = 128
MLA_ROPE = 64
MLA_V = 128
Q_LORA = 768
KV_LORA = 512
MEM_HEADS = 4
QK_W = KV_LORA + 128
SB_SCALE = HEAD_DIM ** -0.5
MLA_SCALE = (MLA_NOPE + MLA_ROPE) ** -0.5
LANE = 128

VMEM_LIMIT_BYTES = 56 * 1024 * 1024


def _params(n_grid_dims):
    return pltpu.CompilerParams(dimension_semantics=("arbitrary",) * n_grid_dims,
                                vmem_limit_bytes=VMEM_LIMIT_BYTES)


def _resident(shape):
    nd = len(shape)
    return pl.BlockSpec(shape, lambda *_: (0,) * nd, pipeline_mode=pl.Buffered(1))


def _rms(x, g):
    return x * lax.rsqrt(jnp.mean(x * x, axis=-1, keepdims=True) + EPS) * g


def _dot(a, b):
    return jnp.dot(a, b, preferred_element_type=F32)


def _dot_nt(a, b):
    return lax.dot_general(a, b, (((1,), (1,)), ((), ())), preferred_element_type=F32)


def _softplus(z):
    return jnp.maximum(z, 0.0) + jnp.log(1.0 + jnp.exp(-jnp.abs(z)))


def _split_bf16(x):
    hi = x.astype(BF16)
    lo = (x - hi.astype(F32)).astype(BF16)
    return hi, lo


def _proj_kernel(x_ref, g_ref, wmain_ref, wkpe_ref, qan_ref, wqup_ref, kvan_ref, wuk_ref,
                 cos_ref, sina_ref, sinb_ref,
                 sbq_ref, sbk_ref, sbv_ref, sbkb_ref, sbvb_ref, ckv_ref, kpe_ref, kcat_ref, qcat_ref):
    h = _rms(x_ref[...], g_ref[...]).astype(BF16)
    p = _dot(h, wmain_ref[...])
    sbq_ref[...] = (p[:, :SB_Q_W] * SB_SCALE).astype(BF16)
    k = p[:, SB_Q_W:SB_Q_W + SB_KV_W]
    v = p[:, SB_Q_W + SB_KV_W:SB_Q_W + 2 * SB_KV_W]
    sbk_ref[...] = k
    sbv_ref[...] = v
    sbkb_ref[...] = k.astype(BF16)
    sbvb_ref[...] = v.astype(BF16)
    o_qc = SB_Q_W + 2 * SB_KV_W
    qc = p[:, o_qc:o_qc + Q_LORA]
    ckv = _rms(p[:, o_qc + Q_LORA:o_qc + Q_LORA + KV_LORA], kvan_ref[...])
    ckv_ref[...] = ckv

    cos = cos_ref[...]
    sina = sina_ref[...]
    sinb = sinb_ref[...]

    def rope(t):
        return t * cos + pltpu.roll(t, LANE - MLA_ROPE // 2, 1) * sina + pltpu.roll(t, MLA_ROPE // 2, 1) * sinb

    kp = rope(_dot(h, wkpe_ref[...]))
    kpe_ref[...] = kp[:, :MLA_ROPE]
    kcat_ref[:, :KV_LORA] = ckv.astype(BF16)
    kcat_ref[:, KV_LORA:] = kp.astype(BF16)

    qn = _rms(qc, qan_ref[...]).astype(BF16)
    q = _dot(qn, wqup_ref[...])
    for hd in range(MLA_HEADS):
        q_nope = q[:, hd * MLA_NOPE:(hd + 1) * MLA_NOPE].astype(BF16)
        q_lat = _dot(q_nope, wuk_ref[hd]) * MLA_SCALE
        qcat_ref[:, hd * QK_W:hd * QK_W + KV_LORA] = q_lat.astype(BF16)
        o = MLA_HEADS * MLA_NOPE + hd * LANE
        q_pe = rope(q[:, o:o + LANE]) * MLA_SCALE
        qcat_ref[:, hd * QK_W + KV_LORA:(hd + 1) * QK_W] = q_pe.astype(BF16)


def _proj(x, pos, w, tm):
    m, d = x.shape
    n_pos_blocks = pos.shape[0] // tm
    half = MLA_ROPE // 2
    inv = ROPE_BASE ** (-jnp.arange(half, dtype=F32) / half)
    ang = pos.astype(F32)[:, None] * inv[None, :]
    cos, sin, zero = jnp.cos(ang), jnp.sin(ang), jnp.zeros_like(ang)
    cos_t = jnp.concatenate([cos, cos, zero, zero], axis=-1)
    sina_t = jnp.concatenate([-sin, zero, zero, zero], axis=-1)
    sinb_t = jnp.concatenate([zero, sin, zero, zero], axis=-1)

    row = lambda width: pl.BlockSpec((tm, width), lambda i: (i, 0))
    tab = pl.BlockSpec((tm, LANE), lambda i: (i % n_pos_blocks, 0))
    out_shapes = (
        jax.ShapeDtypeStruct((m, SB_Q_W), BF16),
        jax.ShapeDtypeStruct((m, SB_KV_W), F32),
        jax.ShapeDtypeStruct((m, SB_KV_W), F32),
        jax.ShapeDtypeStruct((m, SB_KV_W), BF16),
        jax.ShapeDtypeStruct((m, SB_KV_W), BF16),
        jax.ShapeDtypeStruct((m, KV_LORA), F32),
        jax.ShapeDtypeStruct((m, MLA_ROPE), F32),
        jax.ShapeDtypeStruct((m, QK_W), BF16),
        jax.ShapeDtypeStruct((m, MLA_HEADS * QK_W), BF16),
    )
    return pl.pallas_call(
        _proj_kernel,
        grid=(m // tm,),
        in_specs=[row(d), _resident((1, d)), _resident(w["w_main"].shape), _resident(w["w_kpe"].shape),
                  _resident((1, Q_LORA)), _resident(w["w_q_up"].shape), _resident((1, KV_LORA)),
                  _resident(w["w_uk"].shape), tab, tab, tab],
        out_specs=[row(s.shape[1]) for s in out_shapes],
        out_shape=out_shapes,
        compiler_params=_params(1),
        name="mixer_proj",
    )(x, w["norm_mix_pre"], w["w_main"], w["w_kpe"], w["q_a_norm"], w["w_q_up"], w["kv_a_norm"],
      w["w_uk"], cos_t, sina_t, sinb_t)


def _sb_scores(z, mask, tri):
    sp = _softplus(z)
    lom = -sp if mask is None else jnp.where(mask, -sp, 0.0)
    hi, lo = _split_bf16(lom)
    n = z.shape[0]
    suffix = _dot(jnp.concatenate([hi, lo], axis=0), tri)
    suffix = suffix[:n] + suffix[n:]
    return (z - sp) + suffix, suffix[:, :1] + lom[:, :1]


def _sb_prompt_kernel(q_ref, k_ref, v_ref, tri_ref, o_ref, acc_ref, run_ref, *, blk, n_blk):
    i = pl.program_id(2)
    j = pl.program_id(3)

    @pl.when(j == 0)
    def _():
        acc_ref[...] = jnp.zeros_like(acc_ref)
        run_ref[...] = jnp.zeros_like(run_ref)

    @pl.when(j <= i)
    def _():
        rows = SB_GROUP * blk
        q = jnp.concatenate([q_ref[:, g * HEAD_DIM:(g + 1) * HEAD_DIM] for g in range(SB_GROUP)], axis=0)
        z = _dot_nt(q, k_ref[...])
        t_pos = i * blk + lax.broadcasted_iota(jnp.int32, (rows, blk), 0) % blk
        s_pos = (i - j) * blk + lax.broadcasted_iota(jnp.int32, (rows, blk), 1)
        mask = s_pos < t_pos
        logit, total = _sb_scores(z, mask, tri_ref[...])
        a = jnp.where(mask, jnp.exp(logit + run_ref[...]), 0.0)
        acc_ref[...] += _dot(a.astype(BF16), v_ref[...])
        run_ref[...] += total

    @pl.when(j == n_blk - 1)
    def _():
        for g in range(SB_GROUP):
            o_ref[:, g * HEAD_DIM:(g + 1) * HEAD_DIM] = acc_ref[g * blk:(g + 1) * blk, :].astype(o_ref.dtype)


def _tri(n):
    r = lax.broadcasted_iota(jnp.int32, (n, n), 0)
    c = lax.broadcasted_iota(jnp.int32, (n, n), 1)
    return (r > c).astype(BF16)


def _sb_prompt(sbq, sbk, sbv, batch, seq, blk):
    n_blk = seq // blk
    kv_map = lambda b, n, i, j: (b * n_blk + jnp.maximum(i - j, 0), n)
    q_map = lambda b, n, i, j: (b * n_blk + i, n)
    return pl.pallas_call(
        functools.partial(_sb_prompt_kernel, blk=blk, n_blk=n_blk),
        grid=(batch, SB_KV_HEADS, n_blk, n_blk),
        in_specs=[pl.BlockSpec((blk, SB_GROUP * HEAD_DIM), q_map),
                  pl.BlockSpec((blk, HEAD_DIM), kv_map),
                  pl.BlockSpec((blk, HEAD_DIM), kv_map),
                  _resident((blk, blk))],
        out_specs=pl.BlockSpec((blk, SB_GROUP * HEAD_DIM), q_map),
        out_shape=jax.ShapeDtypeStruct((batch * seq, SB_Q_W), BF16),
        scratch_shapes=[pltpu.VMEM((SB_GROUP * blk, HEAD_DIM), F32),
                        pltpu.VMEM((SB_GROUP * blk, 1), F32)],
        compiler_params=_params(4),
        name="sb_prompt",
    )(sbq, sbk, sbv, _tri(blk))


def _sb_sample_kernel(pt_ref, q_ref, knew_ref, vnew_ref, tri_ref, *rest, n_pages_step, n_steps, page, dec):
    del pt_ref
    k_refs = rest[:n_pages_step]
    v_refs = rest[n_pages_step:2 * n_pages_step]
    o_ref, acc_ref, run_ref = rest[2 * n_pages_step:]
    c = pl.program_id(1)
    rows = SB_GROUP * dec
    qf = q_ref[...].astype(F32)
    q = [jnp.concatenate([qf[:, (n * SB_GROUP + g) * HEAD_DIM:(n * SB_GROUP + g + 1) * HEAD_DIM]
                          for g in range(SB_GROUP)], axis=0).astype(BF16) for n in range(SB_KV_HEADS)]
    head = lambda t, n: t[:, n * HEAD_DIM:(n + 1) * HEAD_DIM]

    @pl.when(c == 0)
    def _():
        kn = knew_ref[...]
        vn = vnew_ref[...]
        z = jnp.concatenate([_dot_nt(q[n], head(kn, n)) for n in range(SB_KV_HEADS)], axis=0)
        nr = SB_KV_HEADS * rows
        t_loc = lax.broadcasted_iota(jnp.int32, (nr, page), 0) % dec
        s_loc = lax.broadcasted_iota(jnp.int32, (nr, page), 1)
        mask = s_loc < t_loc
        logit, total = _sb_scores(z, mask, tri_ref[:page, :page])
        a = jnp.where(mask, jnp.exp(logit), 0.0).astype(BF16)
        for n in range(SB_KV_HEADS):
            acc_ref[n] = _dot(a[n * rows:(n + 1) * rows], head(vn, n))
            run_ref[n] = total[n * rows:(n + 1) * rows]

    n_sub = n_pages_step // 2
    kb = [r[...].astype(BF16) for r in k_refs]
    vb = [r[...].astype(BF16) for r in v_refs]
    zs = []
    for n in range(SB_KV_HEADS):
        for s in range(n_sub):
            k2 = jnp.concatenate([head(kb[2 * s], n), head(kb[2 * s + 1], n)], axis=0)
            zs.append(_dot_nt(q[n], k2))
    z = jnp.concatenate(zs, axis=0)
    logit, total = _sb_scores(z, None, tri_ref[...])
    carries = []
    for n in range(SB_KV_HEADS):
        run = run_ref[n]
        per_sub = [None] * n_sub
        for s in reversed(range(n_sub)):
            per_sub[s] = run
            o = (n * n_sub + s) * rows
            run = run + total[o:o + rows]
        run_ref[n] = run
        carries.extend(per_sub)
    a = jnp.exp(logit + jnp.concatenate(carries, axis=0)).astype(BF16)
    for n in range(SB_KV_HEADS):
        a_n = jnp.concatenate([a[(n * n_sub + s) * rows:(n * n_sub + s + 1) * rows] for s in range(n_sub)], axis=1)
        v_n = jnp.concatenate([head(t, n) for t in vb], axis=0)
        acc_ref[n] += _dot(a_n, v_n)

    @pl.when(c == n_steps - 1)
    def _():
        for n in range(SB_KV_HEADS):
            for g in range(SB_GROUP):
                o = (n * SB_GROUP + g) * HEAD_DIM
                o_ref[:, o:o + HEAD_DIM] = acc_ref[n, g * dec:(g + 1) * dec, :]


def _page_specs(n_pages_step, n_steps, page, width):
    def spec(p):
        return pl.BlockSpec((None, page, width),
                            lambda b, c, pt: (pt[b, (n_steps - 1 - c) * n_pages_step + p], 0, 0))
    return [spec(p) for p in range(n_pages_step)]


def _pad_new(x, page):
    return jnp.pad(x, ((0, 0), (0, page - x.shape[1]), (0, 0)))


def _sb_sample(page_table, sbq, sbk_new, sbv_new, cache_k, cache_v, n_pages_step):
    nb, dec, _ = sbq.shape
    n_pages = page_table.shape[1]
    page = cache_k.shape[1]
    n_steps = n_pages // n_pages_step
    per_seq = lambda w, rows: pl.BlockSpec((None, rows, w), lambda b, c, pt: (b, 0, 0))
    kernel = functools.partial(_sb_sample_kernel, n_pages_step=n_pages_step, n_steps=n_steps, page=page, dec=dec)
    grid_spec = pltpu.PrefetchScalarGridSpec(
        num_scalar_prefetch=1,
        grid=(nb, n_steps),
        in_specs=[per_seq(SB_Q_W, dec), per_seq(SB_KV_W, page), per_seq(SB_KV_W, page),
                  pl.BlockSpec((2 * page, 2 * page), lambda b, c, pt: (0, 0))]
                 + _page_specs(n_pages_step, n_steps, page, SB_KV_W) * 2,
        out_specs=per_seq(SB_Q_W, dec),
        scratch_shapes=[pltpu.VMEM((SB_KV_HEADS, SB_GROUP * dec, HEAD_DIM), F32),
                        pltpu.VMEM((SB_KV_HEADS, SB_GROUP * dec, 1), F32)],
    )
    return pl.pallas_call(
        kernel,
        grid_spec=grid_spec,
        out_shape=jax.ShapeDtypeStruct((nb, dec, SB_Q_W), F32),
        compiler_params=_params(2),
        name="sb_sample",
    )(page_table, sbq, _pad_new(sbk_new, page), _pad_new(sbv_new, page), _tri(2 * page),
      *([cache_k] * n_pages_step), *([cache_v] * n_pages_step))


def _online_softmax_step(s, values, m_ref, l_ref, acc_ref):
    m_old = m_ref[...]
    m_new = jnp.maximum(m_old, jnp.max(s, axis=-1, keepdims=True))
    alpha = jnp.exp(m_old - m_new)
    p = jnp.exp(s - m_new)
    l_ref[...] = alpha * l_ref[...] + jnp.sum(p, axis=-1, keepdims=True)
    acc_ref[...] = alpha * acc_ref[...] + _dot(p.astype(BF16), values)
    m_ref[...] = m_new


def _mla_prompt_kernel(q_ref, kv_ref, o_ref, m_ref, l_ref, acc_ref, *, blk, n_blk):
    i = pl.program_id(1)
    j = pl.program_id(2)

    @pl.when(j == 0)
    def _():
        m_ref[...] = jnp.full_like(m_ref, NEG_INF)
        l_ref[...] = jnp.zeros_like(l_ref)
        acc_ref[...] = jnp.zeros_like(acc_ref)

    @pl.when(j <= i)
    def _():
        rows = MLA_HEADS * blk
        q = jnp.concatenate([q_ref[:, hd * QK_W:(hd + 1) * QK_W] for hd in range(MLA_HEADS)], axis=0)
        kv = kv_ref[...]
        s = _dot_nt(q, kv)
        t_pos = i * blk + lax.broadcasted_iota(jnp.int32, (rows, blk), 0) % blk
        s_pos = (i - j) * blk + lax.broadcasted_iota(jnp.int32, (rows, blk), 1)
        s = jnp.where(s_pos <= t_pos, s, NEG_INF)
        _online_softmax_step(s, kv[:, :KV_LORA], m_ref, l_ref, acc_ref)

    @pl.when(j == n_blk - 1)
    def _():
        out = acc_ref[...] / l_ref[...]
        for hd in range(MLA_HEADS):
            o_ref[:, hd * KV_LORA:(hd + 1) * KV_LORA] = out[hd * blk:(hd + 1) * blk].astype(o_ref.dtype)


def _mla_prompt(qcat, kcat, batch, seq, blk):
    n_blk = seq // blk
    rows = MLA_HEADS * blk
    return pl.pallas_call(
        functools.partial(_mla_prompt_kernel, blk=blk, n_blk=n_blk),
        grid=(batch, n_blk, n_blk),
        in_specs=[pl.BlockSpec((blk, MLA_HEADS * QK_W), lambda b, i, j: (b * n_blk + i, 0)),
                  pl.BlockSpec((blk, QK_W), lambda b, i, j: (b * n_blk + jnp.maximum(i - j, 0), 0))],
        out_specs=pl.BlockSpec((blk, MLA_HEADS * KV_LORA), lambda b, i, j: (b * n_blk + i, 0)),
        out_shape=jax.ShapeDtypeStruct((batch * seq, MLA_HEADS * KV_LORA), BF16),
        scratch_shapes=[pltpu.VMEM((rows, 1), F32), pltpu.VMEM((rows, 1), F32),
                        pltpu.VMEM((rows, KV_LORA), F32)],
        compiler_params=_params(3),
        name="mla_prompt",
    )(qcat, kcat)


def _mla_sample_kernel(pt_ref, q_ref, kvnew_ref, *rest, n_pages_step, n_steps, page, dec):
    del pt_ref
    ckv_refs = rest[:n_pages_step]
    kpe_refs = rest[n_pages_step:2 * n_pages_step]
    o_ref, m_ref, l_ref, acc_ref = rest[2 * n_pages_step:]
    c = pl.program_id(1)
    rows = MLA_HEADS * dec
    qf = q_ref[...].astype(F32)
    q = jnp.concatenate([qf[:, hd * QK_W:(hd + 1) * QK_W] for hd in range(MLA_HEADS)], axis=0).astype(BF16)

    @pl.when(c == 0)
    def _():
        kv = kvnew_ref[...]
        s = _dot_nt(q, kv)
        t_loc = lax.broadcasted_iota(jnp.int32, (rows, page), 0) % dec
        s_loc = lax.broadcasted_iota(jnp.int32, (rows, page), 1)
        s = jnp.where(s_loc <= t_loc, s, NEG_INF)
        m = jnp.max(s, axis=-1, keepdims=True)
        p = jnp.exp(s - m)
        m_ref[...] = m
        l_ref[...] = jnp.sum(p, axis=-1, keepdims=True)
        acc_ref[...] = _dot(p.astype(BF16), kv[:, :KV_LORA])

    q_lat = q[:, :KV_LORA]
    q_pe = q[:, KV_LORA:KV_LORA + MLA_ROPE]
    ckv = jnp.concatenate([r[...].astype(BF16) for r in ckv_refs], axis=0)
    kpe = jnp.concatenate([r[...].astype(BF16) for r in kpe_refs], axis=0)
    s = _dot_nt(q_lat, ckv) + _dot_nt(q_pe, kpe)
    _online_softmax_step(s, ckv, m_ref, l_ref, acc_ref)

    @pl.when(c == n_steps - 1)
    def _():
        out = acc_ref[...] / l_ref[...]
        for hd in range(MLA_HEADS):
            o_ref[:, hd * KV_LORA:(hd + 1) * KV_LORA] = out[hd * dec:(hd + 1) * dec]


def _mla_sample(page_table, qcat, kcat_new, cache_ckv, cache_kpe, n_pages_step):
    nb, dec, _ = qcat.shape
    n_pages = page_table.shape[1]
    page = cache_ckv.shape[1]
    n_steps = n_pages // n_pages_step
    rows = MLA_HEADS * dec
    per_seq = lambda w, r: pl.BlockSpec((None, r, w), lambda b, c, pt: (b, 0, 0))
    kernel = functools.partial(_mla_sample_kernel, n_pages_step=n_pages_step, n_steps=n_steps, page=page, dec=dec)
    grid_spec = pltpu.PrefetchScalarGridSpec(
        num_scalar_prefetch=1,
        grid=(nb, n_steps),
        in_specs=[per_seq(MLA_HEADS * QK_W, dec), per_seq(QK_W, page)]
                 + _page_specs(n_pages_step, n_steps, page, KV_LORA)
                 + _page_specs(n_pages_step, n_steps, page, MLA_ROPE),
        out_specs=per_seq(MLA_HEADS * KV_LORA, dec),
        scratch_shapes=[pltpu.VMEM((rows, 1), F32), pltpu.VMEM((rows, 1), F32),
                        pltpu.VMEM((rows, KV_LORA), F32)],
    )
    return pl.pallas_call(
        kernel,
        grid_spec=grid_spec,
        out_shape=jax.ShapeDtypeStruct((nb, dec, MLA_HEADS * KV_LORA), F32),
        compiler_params=_params(2),
        name="mla_sample",
    )(page_table, qcat, _pad_new(kcat_new, page), *([cache_ckv] * n_pages_step), *([cache_kpe] * n_pages_step))


def _mix_out_kernel(sbo_ref, lat_ref, x_ref, wuv_ref, wout_ref, gpost_ref, gxpre_ref, wxq_ref,
                    x1_ref, qm_ref, *, mem_scale):
    lat = lat_ref[...].astype(BF16)
    mla_o = jnp.concatenate(
        [_dot(lat[:, hd * KV_LORA:(hd + 1) * KV_LORA], wuv_ref[hd]) for hd in range(MLA_HEADS)], axis=1)
    y = _dot(sbo_ref[...].astype(BF16), wout_ref[:SB_Q_W, :]) + _dot(mla_o.astype(BF16), wout_ref[SB_Q_W:, :])
    x1 = x_ref[...] + _rms(y, gpost_ref[...])
    x1_ref[...] = x1
    qm = _dot(_rms(x1, gxpre_ref[...]).astype(BF16), wxq_ref[...]) * mem_scale
    qm_ref[...] = qm.astype(qm_ref.dtype)


def _mix_out(sbo, lat, x, w, tm):
    m, d = x.shape
    row = lambda width: pl.BlockSpec((tm, width), lambda i: (i, 0))
    mem_scale = (d // MEM_HEADS) ** -0.5
    return pl.pallas_call(
        functools.partial(_mix_out_kernel, mem_scale=mem_scale),
        grid=(m // tm,),
        in_specs=[row(sbo.shape[1]), row(lat.shape[1]), row(d), _resident(w["w_uv"].shape),
                  _resident(w["w_out"].shape), _resident((1, d)), _resident((1, d)), _resident(w["w_xq"].shape)],
        out_specs=[row(d), row(d)],
        out_shape=(jax.ShapeDtypeStruct((m, d), F32), jax.ShapeDtypeStruct((m, d), BF16)),
        compiler_params=_params(1),
        name="mixer_out",
    )(sbo, lat, x, w["w_uv"], w["w_out"], w["norm_mix_post"], w["norm_x_pre"], w["w_xq"])


def _norm_matmul_kernel(x_ref, g_ref, w_ref, o_ref, ob_ref, h_ref):
    @pl.when(pl.program_id(1) == 0)
    def _():
        h_ref[...] = _rms(x_ref[...], g_ref[...]).astype(BF16)

    y = _dot(h_ref[...], w_ref[...])
    o_ref[...] = y
    ob_ref[...] = y.astype(BF16)


def _norm_matmul(x, g, w, tm, tn):
    m, d = x.shape
    n = w.shape[1]
    return pl.pallas_call(
        _norm_matmul_kernel,
        grid=(m // tm, n // tn),
        in_specs=[pl.BlockSpec((tm, d), lambda i, j: (i, 0)), pl.BlockSpec((1, d), lambda i, j: (0, 0)),
                  pl.BlockSpec((d, tn), lambda i, j: (0, j))],
        out_specs=[pl.BlockSpec((tm, tn), lambda i, j: (i, j))] * 2,
        out_shape=(jax.ShapeDtypeStruct((m, n), F32), jax.ShapeDtypeStruct((m, n), BF16)),
        scratch_shapes=[pltpu.VMEM((tm, d), BF16)],
        compiler_params=_params(2),
        name="memory_kv",
    )(x, g, w)


def _mem_attn_kernel(q_ref, k_ref, v_ref, o_ref):
    hd_w = q_ref.shape[-1] // MEM_HEADS
    q = q_ref[...]
    for hd in range(MEM_HEADS):
        sl = slice(hd * hd_w, (hd + 1) * hd_w)
        s = _dot_nt(q[:, sl], k_ref[:, sl].astype(BF16))
        p = jnp.exp(s - jnp.max(s, axis=-1, keepdims=True))
        o = _dot(p.astype(BF16), v_ref[:, sl].astype(BF16)) / jnp.sum(p, axis=-1, keepdims=True)
        o_ref[:, sl] = o.astype(o_ref.dtype)


def _mem_attn(q, k, v, tq):
    nb, t, d = q.shape
    mt = k.shape[1]
    return pl.pallas_call(
        _mem_attn_kernel,
        grid=(nb, t // tq),
        in_specs=[pl.BlockSpec((None, tq, d), lambda b, i: (b, i, 0)),
                  pl.BlockSpec((None, mt, d), lambda b, i: (b, 0, 0)),
                  pl.BlockSpec((None, mt, d), lambda b, i: (b, 0, 0))],
        out_specs=pl.BlockSpec((None, tq, d), lambda b, i: (b, i, 0)),
        out_shape=jax.ShapeDtypeStruct((nb, t, d), q.dtype),
        compiler_params=_params(2),
        name="memory_attn",
    )(q, k, v)


def _out_proj_kernel(a_ref, x_ref, w_ref, g_ref, o_ref):
    o_ref[...] = x_ref[...] + _rms(_dot(a_ref[...].astype(BF16), w_ref[...]), g_ref[...])


def _out_proj(a, x, w, g, tm):
    m, d = x.shape
    row = lambda width: pl.BlockSpec((tm, width), lambda i: (i, 0))
    return pl.pallas_call(
        _out_proj_kernel,
        grid=(m // tm,),
        in_specs=[row(a.shape[1]), row(d), _resident(w.shape), _resident((1, d))],
        out_specs=row(d),
        out_shape=jax.ShapeDtypeStruct((m, d), F32),
        compiler_params=_params(1),
        name="cross_out",
    )(a, x, w, g)


def _mlp_kernel(x_ref, gpre_ref, wup_ref, wdown_ref, gpost_ref, o_ref, h_ref, acc_ref):
    f = pl.program_id(1)

    @pl.when(f == 0)
    def _():
        h_ref[...] = _rms(x_ref[...], gpre_ref[...]).astype(BF16)
        acc_ref[...] = jnp.zeros_like(acc_ref)

    u = jnp.maximum(_dot(h_ref[...], wup_ref[...]), 0.0)
    acc_ref[...] += _dot((u * u).astype(BF16), wdown_ref[...])

    @pl.when(f == pl.num_programs(1) - 1)
    def _():
        o_ref[...] = x_ref[...] + _rms(acc_ref[...], gpost_ref[...])


def _mlp(x, gpre, w_up, w_down, gpost, tm, tf):
    m, d = x.shape
    dff = w_up.shape[1]
    return pl.pallas_call(
        _mlp_kernel,
        grid=(m // tm, dff // tf),
        in_specs=[pl.BlockSpec((tm, d), lambda i, f: (i, 0)), pl.BlockSpec((1, d), lambda i, f: (0, 0)),
                  pl.BlockSpec((d, tf), lambda i, f: (0, f)), pl.BlockSpec((tf, d), lambda i, f: (f, 0)),
                  pl.BlockSpec((1, d), lambda i, f: (0, 0))],
        out_specs=pl.BlockSpec((tm, d), lambda i, f: (i, 0)),
        out_shape=jax.ShapeDtypeStruct((m, d), F32),
        scratch_shapes=[pltpu.VMEM((tm, d), BF16), pltpu.VMEM((tm, d), F32)],
        compiler_params=_params(2),
        name="mlp",
    )(x, gpre, w_up, w_down, gpost)


def _prepare_weights(norm_mix_pre, norm_mix_post, w_in, q_a_norm, w_q_up, kv_a_norm, w_kv_up, w_out, norm_mem,
                     norm_x_pre, norm_x_post, w_xq, w_xk, w_xv, w_xo, norm_ffn_pre, norm_ffn_post, w_up, w_down):
    d = w_in.shape[0]
    n_main = SB_Q_W + 2 * SB_KV_W + Q_LORA + KV_LORA
    wq = w_q_up.reshape(Q_LORA, MLA_HEADS, MLA_NOPE + MLA_ROPE)
    wq_nope = wq[:, :, :MLA_NOPE].reshape(Q_LORA, MLA_HEADS * MLA_NOPE)
    wq_rope = jnp.pad(wq[:, :, MLA_NOPE:], ((0, 0), (0, 0), (0, LANE - MLA_ROPE))).reshape(Q_LORA, MLA_HEADS * LANE)
    w_kv = w_kv_up.reshape(KV_LORA, MLA_HEADS, MLA_NOPE + MLA_V)
    vec = lambda g: g.reshape(1, -1).astype(F32)
    return {
        "norm_mix_pre": vec(norm_mix_pre), "norm_mix_post": vec(norm_mix_post), "q_a_norm": vec(q_a_norm),
        "kv_a_norm": vec(kv_a_norm), "norm_mem": vec(norm_mem), "norm_x_pre": vec(norm_x_pre),
        "norm_x_post": vec(norm_x_post), "norm_ffn_pre": vec(norm_ffn_pre), "norm_ffn_post": vec(norm_ffn_post),
        "w_main": w_in[:, :n_main].astype(BF16),
        "w_kpe": jnp.pad(w_in[:, n_main:], ((0, 0), (0, LANE - MLA_ROPE))).astype(BF16),
        "w_q_up": jnp.concatenate([wq_nope, wq_rope], axis=1).astype(BF16),
        "w_uk": jnp.transpose(w_kv[:, :, :MLA_NOPE], (1, 2, 0)).astype(BF16),
        "w_uv": jnp.transpose(w_kv[:, :, MLA_NOPE:], (1, 0, 2)).astype(BF16),
        "w_out": w_out.astype(BF16), "w_xq": w_xq.astype(BF16), "w_xk": w_xk.astype(BF16),
        "w_xv": w_xv.astype(BF16), "w_xo": w_xo.astype(BF16), "w_up": w_up.astype(BF16),
        "w_down": w_down.astype(BF16),
    }


def _row_tile(m, cap):
    t = min(m, cap)
    while m % t:
        t //= 2
    return t


def _tail(x1, attn, w, tm, tf):
    x2 = _out_proj(attn, x1, w["w_xo"], w["norm_x_post"], tm)
    return _mlp(x2, w["norm_ffn_pre"], w["w_up"], w["w_down"], w["norm_ffn_post"], tm, tf)


def kernel(x_prompt, x_sample, mem_prompt, cache_sb_k, cache_sb_v, cache_mla_ckv, cache_mla_kpe, cache_mem_k, cache_mem_v, page_table, norm_mix_pre, norm_mix_post, w_in, q_a_norm, w_q_up, kv_a_norm, w_kv_up, w_out, norm_mem, norm_x_pre, norm_x_post, w_xq, w_xk, w_xv, w_xo, norm_ffn_pre, norm_ffn_post, w_up, w_down):
    depth = w_in.shape[0]
    assert depth == 1, "single-layer step"
    batch, seq, d = x_prompt.shape
    nb, dec, _ = x_sample.shape
    mem_t = mem_prompt.shape[1]
    page = cache_sb_k.shape[2]
    n_pages = page_table.shape[1]
    past_len = n_pages * page
    w = _prepare_weights(*(a[0] for a in (norm_mix_pre, norm_mix_post, w_in, q_a_norm, w_q_up, kv_a_norm, w_kv_up,
                                          w_out, norm_mem, norm_x_pre, norm_x_post, w_xq, w_xk, w_xv, w_xo,
                                          norm_ffn_pre, norm_ffn_post, w_up, w_down)))
    blk = _row_tile(seq, 256)
    tf = _row_tile(w_up.shape[2], 1024)
    pages_step = _row_tile(n_pages, 8)

    mp = batch * seq
    tm = _row_tile(mp, 256)
    xp = x_prompt.reshape(mp, d)
    (sbq, sbk, sbv, sbk_b, sbv_b, ckv, kpe, kcat, qcat) = _proj(xp, jnp.arange(seq, dtype=jnp.int32), w, tm)
    sbo = _sb_prompt(sbq, sbk_b, sbv_b, batch, seq, blk)
    lat = _mla_prompt(qcat, kcat, batch, seq, blk)
    x1, qm = _mix_out(sbo, lat, xp, w, tm)
    mem = mem_prompt.reshape(batch * mem_t, d)
    tmm = _row_tile(batch * mem_t, 512)
    mk, mk_b = _norm_matmul(mem, w["norm_mem"], w["w_xk"], tmm, 1024)
    mv, mv_b = _norm_matmul(mem, w["norm_mem"], w["w_xv"], tmm, 1024)
    attn = _mem_attn(qm.reshape(batch, seq, d), mk_b.reshape(batch, mem_t, d), mv_b.reshape(batch, mem_t, d), blk)
    y_prompt = _tail(x1, attn.reshape(mp, d), w, _row_tile(mp, 512), tf).reshape(batch, seq, d)
    prompt_caches = (sbk.reshape(1, batch, seq, SB_KV_HEADS, HEAD_DIM), sbv.reshape(1, batch, seq, SB_KV_HEADS, HEAD_DIM),
                     ckv.reshape(1, batch, seq, KV_LORA), kpe.reshape(1, batch, seq, MLA_ROPE),
                     mk.reshape(1, batch, mem_t, MEM_HEADS, d // MEM_HEADS),
                     mv.reshape(1, batch, mem_t, MEM_HEADS, d // MEM_HEADS))

    ms = nb * dec
    tms = _row_tile(ms, 256)
    xs = x_sample.reshape(ms, d)
    pos_s = past_len + jnp.arange(dec, dtype=jnp.int32)
    pos_tab = jnp.tile(pos_s, tms // dec)
    (sbq, sbk, sbv, sbk_b, sbv_b, ckv, kpe, kcat, qcat) = _proj(xs, pos_tab, w, tms)
    n_pool = cache_sb_k.shape[1]
    sbo = _sb_sample(page_table, sbq.reshape(nb, dec, -1), sbk_b.reshape(nb, dec, -1), sbv_b.reshape(nb, dec, -1),
                     cache_sb_k.reshape(n_pool, page, SB_KV_W), cache_sb_v.reshape(n_pool, page, SB_KV_W), pages_step)
    lat = _mla_sample(page_table, qcat.reshape(nb, dec, -1), kcat.reshape(nb, dec, -1),
                      cache_mla_ckv.reshape(n_pool, page, KV_LORA), cache_mla_kpe.reshape(n_pool, page, MLA_ROPE),
                      pages_step)
    x1, qm = _mix_out(sbo.reshape(ms, -1), lat.reshape(ms, -1), xs, w, tms)
    attn = _mem_attn(qm.reshape(nb, dec, d), cache_mem_k.reshape(nb, mem_t, d), cache_mem_v.reshape(nb, mem_t, d), dec)
    y_sample = _tail(x1, attn.reshape(ms, d), w, _row_tile(ms, 512), tf).reshape(nb, dec, d)
    sample_caches = (sbk.reshape(1, nb, dec, SB_KV_HEADS, HEAD_DIM), sbv.reshape(1, nb, dec, SB_KV_HEADS, HEAD_DIM),
                     ckv.reshape(1, nb, dec, KV_LORA), kpe.reshape(1, nb, dec, MLA_ROPE))

    return (y_prompt, y_sample) + prompt_caches + sample_caches
```

```python
import functools

import jax
import jax.numpy as jnp
from jax import lax
from jax.experimental import pallas as pl
from jax.experimental.pallas import tpu as pltpu

F32 = jnp.float32
BF16 = jnp.bfloat16

EPS = 1e-6
NEG_INF = -1e30
ROPE_BASE = 10000.0
HEAD_DIM = 128
SB_KV_HEADS = 2
SB_GROUP = 4
SB_Q_W = SB_KV_HEADS * SB_GROUP * HEAD_DIM
SB_KV_W = SB_KV_HEADS * HEAD_DIM
MLA_HEADS = 8
MLA_NOPE = 128
MLA_ROPE = 64
MLA_V = 128
Q_LORA = 768
KV_LORA = 512
MEM_HEADS = 4
QK_W = KV_LORA + 128
SB_SCALE = HEAD_DIM ** -0.5
MLA_SCALE = (MLA_NOPE + MLA_ROPE) ** -0.5
LANE = 128
SB_CHAIN_PAGES = 8
MLA_CHAIN_PAGES = 8

VMEM_LIMIT_BYTES = 56 * 1024 * 1024


def _params(n_grid_dims):
    return pltpu.CompilerParams(dimension_semantics=("arbitrary",) * n_grid_dims,
                                vmem_limit_bytes=VMEM_LIMIT_BYTES)


def _resident(shape):
    nd = len(shape)
    return pl.BlockSpec(shape, lambda *_: (0,) * nd, pipeline_mode=pl.Buffered(1))


def _rms(x, g):
    return x * lax.rsqrt(jnp.mean(x * x, axis=-1, keepdims=True) + EPS) * g


def _dot(a, b):
    return jnp.dot(a, b, preferred_element_type=F32)


def _dot_nt(a, b):
    return lax.dot_general(a, b, (((1,), (1,)), ((), ())), preferred_element_type=F32)


def _softplus(z):
    return jnp.maximum(z, 0.0) + jnp.log(1.0 + jnp.exp(-jnp.abs(z)))


def _split_bf16(x):
    hi = x.astype(BF16)
    lo = (x - hi.astype(F32)).astype(BF16)
    return hi, lo


def _proj_kernel(x_ref, g_ref, wmain_ref, wkpe_ref, qan_ref, wqup_ref, kvan_ref, wuk_ref,
                 cos_ref, sina_ref, sinb_ref,
                 sbq_ref, sbk_ref, sbv_ref, sbkb_ref, sbvb_ref, ckv_ref, kpe_ref, kcat_ref, qcat_ref):
    h = _rms(x_ref[...], g_ref[...]).astype(BF16)
    p = _dot(h, wmain_ref[...])
    sbq_ref[...] = (p[:, :SB_Q_W] * SB_SCALE).astype(BF16)
    k = p[:, SB_Q_W:SB_Q_W + SB_KV_W]
    v = p[:, SB_Q_W + SB_KV_W:SB_Q_W + 2 * SB_KV_W]
    sbk_ref[...] = k
    sbv_ref[...] = v
    sbkb_ref[...] = k.astype(BF16)
    sbvb_ref[...] = v.astype(BF16)
    o_qc = SB_Q_W + 2 * SB_KV_W
    qc = p[:, o_qc:o_qc + Q_LORA]
    ckv = _rms(p[:, o_qc + Q_LORA:o_qc + Q_LORA + KV_LORA], kvan_ref[...])
    ckv_ref[...] = ckv

    cos = cos_ref[...]
    sina = sina_ref[...]
    sinb = sinb_ref[...]

    def rope(t):
        return t * cos + pltpu.roll(t, LANE - MLA_ROPE // 2, 1) * sina + pltpu.roll(t, MLA_ROPE // 2, 1) * sinb

    kp = rope(_dot(h, wkpe_ref[...]))
    kpe_ref[...] = kp[:, :MLA_ROPE]
    kcat_ref[:, :KV_LORA] = ckv.astype(BF16)
    kcat_ref[:, KV_LORA:] = kp.astype(BF16)

    qn = _rms(qc, qan_ref[...]).astype(BF16)
    q = _dot(qn, wqup_ref[...])
    for hd in range(MLA_HEADS):
        q_nope = q[:, hd * MLA_NOPE:(hd + 1) * MLA_NOPE].astype(BF16)
        q_lat = _dot(q_nope, wuk_ref[hd]) * MLA_SCALE
        qcat_ref[:, hd * QK_W:hd * QK_W + KV_LORA] = q_lat.astype(BF16)
        o = MLA_HEADS * MLA_NOPE + hd * LANE
        q_pe = rope(q[:, o:o + LANE]) * MLA_SCALE
        qcat_ref[:, hd * QK_W + KV_LORA:(hd + 1) * QK_W] = q_pe.astype(BF16)


def _proj(x, pos, w, tm):
    m, d = x.shape
    n_pos_blocks = pos.shape[0] // tm
    half = MLA_ROPE // 2
    inv = ROPE_BASE ** (-jnp.arange(half, dtype=F32) / half)
    ang = pos.astype(F32)[:, None] * inv[None, :]
    cos, sin, zero = jnp.cos(ang), jnp.sin(ang), jnp.zeros_like(ang)
    cos_t = jnp.concatenate([cos, cos, zero, zero], axis=-1)
    sina_t = jnp.concatenate([-sin, zero, zero, zero], axis=-1)
    sinb_t = jnp.concatenate([zero, sin, zero, zero], axis=-1)

    row = lambda width: pl.BlockSpec((tm, width), lambda i: (i, 0))
    tab = pl.BlockSpec((tm, LANE), lambda i: (i % n_pos_blocks, 0))
    out_shapes = (
        jax.ShapeDtypeStruct((m, SB_Q_W), BF16),
        jax.ShapeDtypeStruct((m, SB_KV_W), F32),
        jax.ShapeDtypeStruct((m, SB_KV_W), F32),
        jax.ShapeDtypeStruct((m, SB_KV_W), BF16),
        jax.ShapeDtypeStruct((m, SB_KV_W), BF16),
        jax.ShapeDtypeStruct((m, KV_LORA), F32),
        jax.ShapeDtypeStruct((m, MLA_ROPE), F32),
        jax.ShapeDtypeStruct((m, QK_W), BF16),
        jax.ShapeDtypeStruct((m, MLA_HEADS * QK_W), BF16),
    )
    return pl.pallas_call(
        _proj_kernel,
        grid=(m // tm,),
        in_specs=[row(d), _resident((1, d)), _resident(w["w_main"].shape), _resident(w["w_kpe"].shape),
                  _resident((1, Q_LORA)), _resident(w["w_q_up"].shape), _resident((1, KV_LORA)),
                  _resident(w["w_uk"].shape), tab, tab, tab],
        out_specs=[row(s.shape[1]) for s in out_shapes],
        out_shape=out_shapes,
        compiler_params=_params(1),
        name="mixer_proj",
    )(x, w["norm_mix_pre"], w["w_main"], w["w_kpe"], w["q_a_norm"], w["w_q_up"], w["kv_a_norm"],
      w["w_uk"], cos_t, sina_t, sinb_t)


def _sb_scores(z, mask, tri):
    sp = _softplus(z)
    lom = -sp if mask is None else jnp.where(mask, -sp, 0.0)
    hi, lo = _split_bf16(lom)
    n = z.shape[0]
    suffix = _dot(jnp.concatenate([hi, lo], axis=0), tri)
    suffix = suffix[:n] + suffix[n:]
    return (z - sp) + suffix, suffix[:, :1] + lom[:, :1]


def _sb_prompt_kernel(q_ref, k_ref, v_ref, tri_ref, o_ref, acc_ref, run_ref, *, blk, n_blk):
    i = pl.program_id(2)
    j = pl.program_id(3)

    @pl.when(j == 0)
    def _():
        acc_ref[...] = jnp.zeros_like(acc_ref)
        run_ref[...] = jnp.zeros_like(run_ref)

    @pl.when(j <= i)
    def _():
        rows = SB_GROUP * blk
        q = jnp.concatenate([q_ref[:, g * HEAD_DIM:(g + 1) * HEAD_DIM] for g in range(SB_GROUP)], axis=0)
        z = _dot_nt(q, k_ref[...])
        t_pos = i * blk + lax.broadcasted_iota(jnp.int32, (rows, blk), 0) % blk
        s_pos = (i - j) * blk + lax.broadcasted_iota(jnp.int32, (rows, blk), 1)
        mask = s_pos < t_pos
        logit, total = _sb_scores(z, mask, tri_ref[...])
        a = jnp.where(mask, jnp.exp(logit + run_ref[...]), 0.0)
        acc_ref[...] += _dot(a.astype(BF16), v_ref[...])
        run_ref[...] += total

    @pl.when(j == n_blk - 1)
    def _():
        for g in range(SB_GROUP):
            o_ref[:, g * HEAD_DIM:(g + 1) * HEAD_DIM] = acc_ref[g * blk:(g + 1) * blk, :].astype(o_ref.dtype)


def _tri(n):
    r = lax.broadcasted_iota(jnp.int32, (n, n), 0)
    c = lax.broadcasted_iota(jnp.int32, (n, n), 1)
    return (r > c).astype(BF16)


def _sb_prompt(sbq, sbk, sbv, batch, seq, blk):
    n_blk = seq // blk
    kv_map = lambda b, n, i, j: (b * n_blk + jnp.maximum(i - j, 0), n)
    q_map = lambda b, n, i, j: (b * n_blk + i, n)
    return pl.pallas_call(
        functools.partial(_sb_prompt_kernel, blk=blk, n_blk=n_blk),
        grid=(batch, SB_KV_HEADS, n_blk, n_blk),
        in_specs=[pl.BlockSpec((blk, SB_GROUP * HEAD_DIM), q_map),
                  pl.BlockSpec((blk, HEAD_DIM), kv_map),
                  pl.BlockSpec((blk, HEAD_DIM), kv_map),
                  _resident((blk, blk))],
        out_specs=pl.BlockSpec((blk, SB_GROUP * HEAD_DIM), q_map),
        out_shape=jax.ShapeDtypeStruct((batch * seq, SB_Q_W), BF16),
        scratch_shapes=[pltpu.VMEM((SB_GROUP * blk, HEAD_DIM), F32),
                        pltpu.VMEM((SB_GROUP * blk, 1), F32)],
        compiler_params=_params(4),
        name="sb_prompt",
    )(sbq, sbk, sbv, _tri(blk))


def _sb_sample_kernel(pt_ref, q_ref, knew_ref, vnew_ref, tri_ref, *rest, n_pages_step, n_steps, page, dec):
    del pt_ref
    k_refs = rest[:n_pages_step]
    v_refs = rest[n_pages_step:2 * n_pages_step]
    o_ref, acc_ref, run_ref = rest[2 * n_pages_step:]
    c = pl.program_id(1)
    rows = SB_GROUP * dec
    qf = q_ref[...].astype(F32)
    q = [jnp.concatenate([qf[:, (n * SB_GROUP + g) * HEAD_DIM:(n * SB_GROUP + g + 1) * HEAD_DIM]
                          for g in range(SB_GROUP)], axis=0).astype(BF16) for n in range(SB_KV_HEADS)]
    head = lambda t, n: t[:, n * HEAD_DIM:(n + 1) * HEAD_DIM]

    @pl.when(c == 0)
    def _():
        kn = knew_ref[...]
        vn = vnew_ref[...]
        z = jnp.concatenate([_dot_nt(q[n], head(kn, n)) for n in range(SB_KV_HEADS)], axis=0)
        nr = SB_KV_HEADS * rows
        t_loc = lax.broadcasted_iota(jnp.int32, (nr, page), 0) % dec
        s_loc = lax.broadcasted_iota(jnp.int32, (nr, page), 1)
        mask = s_loc < t_loc
        logit, total = _sb_scores(z, mask, tri_ref[:page, :page])
        a = jnp.where(mask, jnp.exp(logit), 0.0).astype(BF16)
        for n in range(SB_KV_HEADS):
            acc_ref[n] = _dot(a[n * rows:(n + 1) * rows], head(vn, n))
            run_ref[n] = total[n * rows:(n + 1) * rows]

    cached = lambda r, n: r[pl.ds(n, page, stride=SB_KV_HEADS), :].astype(BF16)
    n_sub = SB_CHAIN_PAGES // 2
    n_chain = n_pages_step // SB_CHAIN_PAGES
    logits, totals = [], []
    for ch in range(n_chain):
        refs = k_refs[ch * SB_CHAIN_PAGES:(ch + 1) * SB_CHAIN_PAGES]
        zs = []
        for n in range(SB_KV_HEADS):
            for s in range(n_sub):
                k2 = jnp.concatenate([cached(refs[2 * s], n), cached(refs[2 * s + 1], n)], axis=0)
                zs.append(_dot_nt(q[n], k2))
        logit, total = _sb_scores(jnp.concatenate(zs, axis=0), None, tri_ref[...])
        logits.append(logit)
        totals.append(total)
    run = [run_ref[n] for n in range(SB_KV_HEADS)]
    carries = [[None] * (SB_KV_HEADS * n_sub) for _ in range(n_chain)]
    for ch in reversed(range(n_chain)):
        for n in range(SB_KV_HEADS):
            for s in reversed(range(n_sub)):
                carries[ch][n * n_sub + s] = run[n]
                o = (n * n_sub + s) * rows
                run[n] = run[n] + totals[ch][o:o + rows]
    out = [None] * SB_KV_HEADS
    for ch in range(n_chain):
        a = jnp.exp(logits[ch] + jnp.concatenate(carries[ch], axis=0)).astype(BF16)
        refs = v_refs[ch * SB_CHAIN_PAGES:(ch + 1) * SB_CHAIN_PAGES]
        for n in range(SB_KV_HEADS):
            a_n = jnp.concatenate([a[(n * n_sub + s) * rows:(n * n_sub + s + 1) * rows] for s in range(n_sub)], axis=1)
            v_n = jnp.concatenate([cached(r, n) for r in refs], axis=0)
            o_n = _dot(a_n, v_n)
            out[n] = o_n if out[n] is None else out[n] + o_n
    for n in range(SB_KV_HEADS):
        run_ref[n] = run[n]
        acc_ref[n] += out[n]

    @pl.when(c == n_steps - 1)
    def _():
        for n in range(SB_KV_HEADS):
            for g in range(SB_GROUP):
                o = (n * SB_GROUP + g) * HEAD_DIM
                o_ref[:, o:o + HEAD_DIM] = acc_ref[n, g * dec:(g + 1) * dec, :]


def _page_specs(n_pages_step, n_steps, block_shape):
    zeros = (0,) * (len(block_shape) - 1)

    def spec(p):
        return pl.BlockSpec(block_shape,
                            lambda b, c, pt: (pt[b, (n_steps - 1 - c) * n_pages_step + p],) + zeros)
    return [spec(p) for p in range(n_pages_step)]


def _pad_new(x, page):
    return jnp.pad(x, ((0, 0), (0, page - x.shape[1]), (0, 0)))


def _sb_sample(page_table, sbq, sbk_new, sbv_new, cache_k, cache_v, page, n_pages_step):
    nb, dec, _ = sbq.shape
    n_pages = page_table.shape[1]
    n_steps = n_pages // n_pages_step
    per_seq = lambda w, rows: pl.BlockSpec((None, rows, w), lambda b, c, pt: (b, 0, 0))
    kernel = functools.partial(_sb_sample_kernel, n_pages_step=n_pages_step, n_steps=n_steps, page=page, dec=dec)
    grid_spec = pltpu.PrefetchScalarGridSpec(
        num_scalar_prefetch=1,
        grid=(nb, n_steps),
        in_specs=[per_seq(SB_Q_W, dec), per_seq(SB_KV_W, page), per_seq(SB_KV_W, page),
                  pl.BlockSpec((2 * page, 2 * page), lambda b, c, pt: (0, 0))]
                 + _page_specs(n_pages_step, n_steps, (SB_KV_HEADS * page, HEAD_DIM))
                 + _page_specs(n_pages_step, n_steps, (SB_KV_HEADS * page, HEAD_DIM)),
        out_specs=per_seq(SB_Q_W, dec),
        scratch_shapes=[pltpu.VMEM((SB_KV_HEADS, SB_GROUP * dec, HEAD_DIM), F32),
                        pltpu.VMEM((SB_KV_HEADS, SB_GROUP * dec, 1), F32)],
    )
    return pl.pallas_call(
        kernel,
        grid_spec=grid_spec,
        out_shape=jax.ShapeDtypeStruct((nb, dec, SB_Q_W), F32),
        compiler_params=_params(2),
        name="sb_sample",
    )(page_table, sbq, _pad_new(sbk_new, page), _pad_new(sbv_new, page), _tri(2 * page),
      *([cache_k] * n_pages_step), *([cache_v] * n_pages_step))


def _online_softmax_step(s, values, m_ref, l_ref, acc_ref):
    m_old = m_ref[...]
    m_new = jnp.maximum(m_old, jnp.max(s, axis=-1, keepdims=True))
    alpha = jnp.exp(m_old - m_new)
    p = jnp.exp(s - m_new)
    l_ref[...] = alpha * l_ref[...] + jnp.sum(p, axis=-1, keepdims=True)
    acc_ref[...] = alpha * acc_ref[...] + _dot(p.astype(BF16), values)
    m_ref[...] = m_new


def _mla_prompt_kernel(q_ref, kv_ref, o_ref, m_ref, l_ref, acc_ref, *, bq, bk, n_kb):
    i = pl.program_id(1)
    j = pl.program_id(2)
    last = ((i + 1) * bq - 1) // bk

    @pl.when(j == 0)
    def _():
        m_ref[...] = jnp.full_like(m_ref, NEG_INF)
        l_ref[...] = jnp.zeros_like(l_ref)
        acc_ref[...] = jnp.zeros_like(acc_ref)

    @pl.when(j <= last)
    def _():
        rows = MLA_HEADS * bq
        q = jnp.concatenate([q_ref[:, hd * QK_W:(hd + 1) * QK_W] for hd in range(MLA_HEADS)], axis=0)
        kv = kv_ref[...]
        s = _dot_nt(q, kv)
        t_pos = i * bq + lax.broadcasted_iota(jnp.int32, (rows, bk), 0) % bq
        s_pos = (last - j) * bk + lax.broadcasted_iota(jnp.int32, (rows, bk), 1)
        s = jnp.where(s_pos <= t_pos, s, NEG_INF)
        _online_softmax_step(s, kv[:, :KV_LORA], m_ref, l_ref, acc_ref)

    @pl.when(j == n_kb - 1)
    def _():
        out = acc_ref[...] / l_ref[...]
        for hd in range(MLA_HEADS):
            o_ref[:, hd * KV_LORA:(hd + 1) * KV_LORA] = out[hd * bq:(hd + 1) * bq].astype(o_ref.dtype)


def _mla_prompt(qcat, kcat, batch, seq, bq, bk):
    assert bk % bq == 0
    n_qb = seq // bq
    n_kb = seq // bk
    rows = MLA_HEADS * bq
    kv_map = lambda b, i, j: (b * n_kb + jnp.maximum(((i + 1) * bq - 1) // bk - j, 0), 0)
    return pl.pallas_call(
        functools.partial(_mla_prompt_kernel, bq=bq, bk=bk, n_kb=n_kb),
        grid=(batch, n_qb, n_kb),
        in_specs=[pl.BlockSpec((bq, MLA_HEADS * QK_W), lambda b, i, j: (b * n_qb + i, 0)),
                  pl.BlockSpec((bk, QK_W), kv_map)],
        out_specs=pl.BlockSpec((bq, MLA_HEADS * KV_LORA), lambda b, i, j: (b * n_qb + i, 0)),
        out_shape=jax.ShapeDtypeStruct((batch * seq, MLA_HEADS * KV_LORA), BF16),
        scratch_shapes=[pltpu.VMEM((rows, 1), F32), pltpu.VMEM((rows, 1), F32),
                        pltpu.VMEM((rows, KV_LORA), F32)],
        compiler_params=_params(3),
        name="mla_prompt",
    )(qcat, kcat)


def _mla_sample_kernel(pt_ref, q_ref, kvnew_ref, *rest, n_pages_step, n_steps, page, dec):
    del pt_ref
    ckv_refs = rest[:n_pages_step]
    kpe_refs = rest[n_pages_step:2 * n_pages_step]
    o_ref, m_ref, l_ref, acc_ref = rest[2 * n_pages_step:]
    c = pl.program_id(1)
    rows = MLA_HEADS * dec
    qf = q_ref[...].astype(F32)
    q = jnp.concatenate([qf[:, hd * QK_W:(hd + 1) * QK_W] for hd in range(MLA_HEADS)], axis=0).astype(BF16)

    @pl.when(c == 0)
    def _():
        kv = kvnew_ref[...]
        s = _dot_nt(q, kv)
        t_loc = lax.broadcasted_iota(jnp.int32, (rows, page), 0) % dec
        s_loc = lax.broadcasted_iota(jnp.int32, (rows, page), 1)
        s = jnp.where(s_loc <= t_loc, s, NEG_INF)
        m = jnp.max(s, axis=-1, keepdims=True)
        p = jnp.exp(s - m)
        m_ref[...] = m
        l_ref[...] = jnp.sum(p, axis=-1, keepdims=True)
        acc_ref[...] = _dot(p.astype(BF16), kv[:, :KV_LORA])

    q_lat = q[:, :KV_LORA]
    q_pe = q[:, KV_LORA:KV_LORA + MLA_ROPE]
    parts = []
    for ch in range(n_pages_step // MLA_CHAIN_PAGES):
        sl = slice(ch * MLA_CHAIN_PAGES, (ch + 1) * MLA_CHAIN_PAGES)
        ckv = jnp.concatenate([r[...].astype(BF16) for r in ckv_refs[sl]], axis=0)
        kpe_t = jnp.concatenate([r[...].astype(BF16) for r in kpe_refs[sl]], axis=1)
        s = _dot_nt(q_lat, ckv) + _dot(q_pe, kpe_t)
        m_c = jnp.max(s, axis=-1, keepdims=True)
        p = jnp.exp(s - m_c)
        parts.append((m_c, jnp.sum(p, axis=-1, keepdims=True), _dot(p.astype(BF16), ckv)))
    m_old = m_ref[...]
    m_new = functools.reduce(jnp.maximum, [m_c for m_c, _, _ in parts], m_old)
    alpha = jnp.exp(m_old - m_new)
    l_new = alpha * l_ref[...]
    acc = alpha * acc_ref[...]
    for m_c, l_c, o_c in parts:
        w_c = jnp.exp(m_c - m_new)
        l_new = l_new + w_c * l_c
        acc = acc + w_c * o_c
    m_ref[...] = m_new
    l_ref[...] = l_new
    acc_ref[...] = acc

    @pl.when(c == n_steps - 1)
    def _():
        out = acc_ref[...] / l_ref[...]
        for hd in range(MLA_HEADS):
            o_ref[:, hd * KV_LORA:(hd + 1) * KV_LORA] = out[hd * dec:(hd + 1) * dec]


def _mla_sample(page_table, qcat, kcat_new, cache_ckv, cache_kpe_t, n_pages_step):
    nb, dec, _ = qcat.shape
    n_pages = page_table.shape[1]
    page = cache_ckv.shape[1]
    n_steps = n_pages // n_pages_step
    rows = MLA_HEADS * dec
    per_seq = lambda w, r: pl.BlockSpec((None, r, w), lambda b, c, pt: (b, 0, 0))
    kernel = functools.partial(_mla_sample_kernel, n_pages_step=n_pages_step, n_steps=n_steps, page=page, dec=dec)
    grid_spec = pltpu.PrefetchScalarGridSpec(
        num_scalar_prefetch=1,
        grid=(nb, n_steps),
        in_specs=[per_seq(MLA_HEADS * QK_W, dec), per_seq(QK_W, page)]
                 + _page_specs(n_pages_step, n_steps, (None, page, KV_LORA))
                 + _page_specs(n_pages_step, n_steps, (None, MLA_ROPE, page)),
        out_specs=per_seq(MLA_HEADS * KV_LORA, dec),
        scratch_shapes=[pltpu.VMEM((rows, 1), F32), pltpu.VMEM((rows, 1), F32),
                        pltpu.VMEM((rows, KV_LORA), F32)],
    )
    return pl.pallas_call(
        kernel,
        grid_spec=grid_spec,
        out_shape=jax.ShapeDtypeStruct((nb, dec, MLA_HEADS * KV_LORA), F32),
        compiler_params=_params(2),
        name="mla_sample",
    )(page_table, qcat, _pad_new(kcat_new, page), *([cache_ckv] * n_pages_step), *([cache_kpe_t] * n_pages_step))


def _mix_out_kernel(sbo_ref, lat_ref, x_ref, wuv_ref, wout_ref, gpost_ref, gxpre_ref, wxq_ref,
                    x1_ref, qm_ref, *, mem_scale):
    lat = lat_ref[...].astype(BF16)
    mla_o = jnp.concatenate(
        [_dot(lat[:, hd * KV_LORA:(hd + 1) * KV_LORA], wuv_ref[hd]) for hd in range(MLA_HEADS)], axis=1)
    y = _dot(sbo_ref[...].astype(BF16), wout_ref[:SB_Q_W, :]) + _dot(mla_o.astype(BF16), wout_ref[SB_Q_W:, :])
    x1 = x_ref[...] + _rms(y, gpost_ref[...])
    x1_ref[...] = x1
    qm = _dot(_rms(x1, gxpre_ref[...]).astype(BF16), wxq_ref[...]) * mem_scale
    qm_ref[...] = qm.astype(qm_ref.dtype)


def _mix_out(sbo, lat, x, w, tm):
    m, d = x.shape
    row = lambda width: pl.BlockSpec((tm, width), lambda i: (i, 0))
    mem_scale = (d // MEM_HEADS) ** -0.5
    return pl.pallas_call(
        functools.partial(_mix_out_kernel, mem_scale=mem_scale),
        grid=(m // tm,),
        in_specs=[row(sbo.shape[1]), row(lat.shape[1]), row(d), _resident(w["w_uv"].shape),
                  _resident(w["w_out"].shape), _resident((1, d)), _resident((1, d)), _resident(w["w_xq"].shape)],
        out_specs=[row(d), row(d)],
        out_shape=(jax.ShapeDtypeStruct((m, d), F32), jax.ShapeDtypeStruct((m, d), BF16)),
        compiler_params=_params(1),
        name="mixer_out",
    )(sbo, lat, x, w["w_uv"], w["w_out"], w["norm_mix_post"], w["norm_x_pre"], w["w_xq"])


def _norm_matmul_kernel(x_ref, g_ref, w_ref, o_ref, ob_ref, h_ref):
    @pl.when(pl.program_id(1) == 0)
    def _():
        h_ref[...] = _rms(x_ref[...], g_ref[...]).astype(BF16)

    y = _dot(h_ref[...], w_ref[...])
    o_ref[...] = y
    ob_ref[...] = y.astype(BF16)


def _norm_matmul(x, g, w, tm, tn):
    m, d = x.shape
    n = w.shape[1]
    return pl.pallas_call(
        _norm_matmul_kernel,
        grid=(m // tm, n // tn),
        in_specs=[pl.BlockSpec((tm, d), lambda i, j: (i, 0)), pl.BlockSpec((1, d), lambda i, j: (0, 0)),
                  pl.BlockSpec((d, tn), lambda i, j: (0, j))],
        out_specs=[pl.BlockSpec((tm, tn), lambda i, j: (i, j))] * 2,
        out_shape=(jax.ShapeDtypeStruct((m, n), F32), jax.ShapeDtypeStruct((m, n), BF16)),
        scratch_shapes=[pltpu.VMEM((tm, d), BF16)],
        compiler_params=_params(2),
        name="memory_kv",
    )(x, g, w)


def _mem_attn_kernel(q_ref, k_ref, v_ref, o_ref, *, tiled_rows):
    hd_w = q_ref.shape[-1] // MEM_HEADS
    n_lane_tiles = hd_w // LANE
    q = q_ref[...]

    def head_of(ref, hd):
        if not tiled_rows:
            return ref[:, hd * hd_w:(hd + 1) * hd_w].astype(BF16)
        stride = n_lane_tiles * MEM_HEADS
        mt = ref.shape[0] // stride
        return jnp.concatenate([ref[pl.ds(j * MEM_HEADS + hd, mt, stride=stride), :].astype(BF16)
                                for j in range(n_lane_tiles)], axis=1)

    for hd in range(MEM_HEADS):
        sl = slice(hd * hd_w, (hd + 1) * hd_w)
        s = _dot_nt(q[:, sl], head_of(k_ref, hd))
        p = jnp.exp(s - jnp.max(s, axis=-1, keepdims=True))
        o = _dot(p.astype(BF16), head_of(v_ref, hd)) / jnp.sum(p, axis=-1, keepdims=True)
        o_ref[:, sl] = o.astype(o_ref.dtype)


def _mem_attn(q, k, v, tq, tiled_rows=False):
    nb, t, d = q.shape
    kv_block = (None,) + k.shape[1:]
    return pl.pallas_call(
        functools.partial(_mem_attn_kernel, tiled_rows=tiled_rows),
        grid=(nb, t // tq),
        in_specs=[pl.BlockSpec((None, tq, d), lambda b, i: (b, i, 0)),
                  pl.BlockSpec(kv_block, lambda b, i: (b, 0, 0)),
                  pl.BlockSpec(kv_block, lambda b, i: (b, 0, 0))],
        out_specs=pl.BlockSpec((None, tq, d), lambda b, i: (b, i, 0)),
        out_shape=jax.ShapeDtypeStruct((nb, t, d), q.dtype),
        compiler_params=_params(2),
        name="memory_attn",
    )(q, k, v)


def _out_proj_kernel(a_ref, x_ref, w_ref, g_ref, o_ref):
    o_ref[...] = x_ref[...] + _rms(_dot(a_ref[...].astype(BF16), w_ref[...]), g_ref[...])


def _out_proj(a, x, w, g, tm):
    m, d = x.shape
    row = lambda width: pl.BlockSpec((tm, width), lambda i: (i, 0))
    return pl.pallas_call(
        _out_proj_kernel,
        grid=(m // tm,),
        in_specs=[row(a.shape[1]), row(d), _resident(w.shape), _resident((1, d))],
        out_specs=row(d),
        out_shape=jax.ShapeDtypeStruct((m, d), F32),
        compiler_params=_params(1),
        name="cross_out",
    )(a, x, w, g)


def _mlp_kernel(x_ref, gpre_ref, wup_ref, wdown_ref, gpost_ref, o_ref, h_ref, acc_ref):
    f = pl.program_id(1)

    @pl.when(f == 0)
    def _():
        h_ref[...] = _rms(x_ref[...], gpre_ref[...]).astype(BF16)
        acc_ref[...] = jnp.zeros_like(acc_ref)

    u = jnp.maximum(_dot(h_ref[...], wup_ref[...]), 0.0)
    acc_ref[...] += _dot((u * u).astype(BF16), wdown_ref[...])

    @pl.when(f == pl.num_programs(1) - 1)
    def _():
        o_ref[...] = x_ref[...] + _rms(acc_ref[...], gpost_ref[...])


def _mlp(x, gpre, w_up, w_down, gpost, tm, tf):
    m, d = x.shape
    dff = w_up.shape[1]
    return pl.pallas_call(
        _mlp_kernel,
        grid=(m // tm, dff // tf),
        in_specs=[pl.BlockSpec((tm, d), lambda i, f: (i, 0)), pl.BlockSpec((1, d), lambda i, f: (0, 0)),
                  pl.BlockSpec((d, tf), lambda i, f: (0, f)), pl.BlockSpec((tf, d), lambda i, f: (f, 0)),
                  pl.BlockSpec((1, d), lambda i, f: (0, 0))],
        out_specs=pl.BlockSpec((tm, d), lambda i, f: (i, 0)),
        out_shape=jax.ShapeDtypeStruct((m, d), F32),
        scratch_shapes=[pltpu.VMEM((tm, d), BF16), pltpu.VMEM((tm, d), F32)],
        compiler_params=_params(2),
        name="mlp",
    )(x, gpre, w_up, w_down, gpost)


def _prepare_weights(norm_mix_pre, norm_mix_post, w_in, q_a_norm, w_q_up, kv_a_norm, w_kv_up, w_out, norm_mem,
                     norm_x_pre, norm_x_post, w_xq, w_xk, w_xv, w_xo, norm_ffn_pre, norm_ffn_post, w_up, w_down):
    d = w_in.shape[0]
    n_main = SB_Q_W + 2 * SB_KV_W + Q_LORA + KV_LORA
    wq = w_q_up.reshape(Q_LORA, MLA_HEADS, MLA_NOPE + MLA_ROPE)
    wq_nope = wq[:, :, :MLA_NOPE].reshape(Q_LORA, MLA_HEADS * MLA_NOPE)
    wq_rope = jnp.pad(wq[:, :, MLA_NOPE:], ((0, 0), (0, 0), (0, LANE - MLA_ROPE))).reshape(Q_LORA, MLA_HEADS * LANE)
    w_kv = w_kv_up.reshape(KV_LORA, MLA_HEADS, MLA_NOPE + MLA_V)
    vec = lambda g: g.reshape(1, -1).astype(F32)
    return {
        "norm_mix_pre": vec(norm_mix_pre), "norm_mix_post": vec(norm_mix_post), "q_a_norm": vec(q_a_norm),
        "kv_a_norm": vec(kv_a_norm), "norm_mem": vec(norm_mem), "norm_x_pre": vec(norm_x_pre),
        "norm_x_post": vec(norm_x_post), "norm_ffn_pre": vec(norm_ffn_pre), "norm_ffn_post": vec(norm_ffn_post),
        "w_main": w_in[:, :n_main].astype(BF16),
        "w_kpe": jnp.pad(w_in[:, n_main:], ((0, 0), (0, LANE - MLA_ROPE))).astype(BF16),
        "w_q_up": jnp.concatenate([wq_nope, wq_rope], axis=1).astype(BF16),
        "w_uk": jnp.transpose(w_kv[:, :, :MLA_NOPE], (1, 2, 0)).astype(BF16),
        "w_uv": jnp.transpose(w_kv[:, :, MLA_NOPE:], (1, 0, 2)).astype(BF16),
        "w_out": w_out.astype(BF16), "w_xq": w_xq.astype(BF16), "w_xk": w_xk.astype(BF16),
        "w_xv": w_xv.astype(BF16), "w_xo": w_xo.astype(BF16), "w_up": w_up.astype(BF16),
        "w_down": w_down.astype(BF16),
    }


def _row_tile(m, cap):
    t = min(m, cap)
    while m % t:
        t //= 2
    return t


def _tail(x1, attn, w, tm, tf):
    x2 = _out_proj(attn, x1, w["w_xo"], w["norm_x_post"], tm)
    return _mlp(x2, w["norm_ffn_pre"], w["w_up"], w["w_down"], w["norm_ffn_post"], tm, tf)


def kernel(x_prompt, x_sample, mem_prompt, cache_sb_k, cache_sb_v, cache_mla_ckv, cache_mla_kpe, cache_mem_k, cache_mem_v, page_table, norm_mix_pre, norm_mix_post, w_in, q_a_norm, w_q_up, kv_a_norm, w_kv_up, w_out, norm_mem, norm_x_pre, norm_x_post, w_xq, w_xk, w_xv, w_xo, norm_ffn_pre, norm_ffn_post, w_up, w_down):
    depth = w_in.shape[0]
    assert depth == 1, "single-layer step"
    batch, seq, d = x_prompt.shape
    nb, dec, _ = x_sample.shape
    mem_t = mem_prompt.shape[1]
    page = cache_sb_k.shape[2]
    n_pages = page_table.shape[1]
    past_len = n_pages * page
    w = _prepare_weights(*(a[0] for a in (norm_mix_pre, norm_mix_post, w_in, q_a_norm, w_q_up, kv_a_norm, w_kv_up,
                                          w_out, norm_mem, norm_x_pre, norm_x_post, w_xq, w_xk, w_xv, w_xo,
                                          norm_ffn_pre, norm_ffn_post, w_up, w_down)))
    blk = _row_tile(seq, 256)
    tf = _row_tile(w_up.shape[2], 1024)
    pages_step = _row_tile(n_pages, 32)

    mp = batch * seq
    tm = _row_tile(mp, 256)
    xp = x_prompt.reshape(mp, d)
    (sbq, sbk, sbv, sbk_b, sbv_b, ckv, kpe, kcat, qcat) = _proj(xp, jnp.arange(seq, dtype=jnp.int32), w, tm)
    sbo = _sb_prompt(sbq, sbk_b, sbv_b, batch, seq, blk)
    lat = _mla_prompt(qcat, kcat, batch, seq, _row_tile(seq, 128), _row_tile(seq, 512))
    x1, qm = _mix_out(sbo, lat, xp, w, tm)
    mem = mem_prompt.reshape(batch * mem_t, d)
    tmm = _row_tile(batch * mem_t, 512)
    mk, mk_b = _norm_matmul(mem, w["norm_mem"], w["w_xk"], tmm, 1024)
    mv, mv_b = _norm_matmul(mem, w["norm_mem"], w["w_xv"], tmm, 1024)
    attn = _mem_attn(qm.reshape(batch, seq, d), mk_b.reshape(batch, mem_t, d), mv_b.reshape(batch, mem_t, d), blk)
    y_prompt = _tail(x1, attn.reshape(mp, d), w, _row_tile(mp, 512), tf).reshape(batch, seq, d)
    prompt_caches = (sbk.reshape(1, batch, seq, SB_KV_HEADS, HEAD_DIM), sbv.reshape(1, batch, seq, SB_KV_HEADS, HEAD_DIM),
                     ckv.reshape(1, batch, seq, KV_LORA), kpe.reshape(1, batch, seq, MLA_ROPE),
                     mk.reshape(1, batch, mem_t, MEM_HEADS, d // MEM_HEADS),
                     mv.reshape(1, batch, mem_t, MEM_HEADS, d // MEM_HEADS))

    ms = nb * dec
    tms = _row_tile(ms, 256)
    xs = x_sample.reshape(ms, d)
    pos_s = past_len + jnp.arange(dec, dtype=jnp.int32)
    pos_tab = jnp.tile(pos_s, tms // dec)
    (sbq, sbk, sbv, sbk_b, sbv_b, ckv, kpe, kcat, qcat) = _proj(xs, pos_tab, w, tms)
    n_pool = cache_sb_k.shape[1]
    sb_rows = lambda cache: cache.reshape(n_pool * page * SB_KV_HEADS, HEAD_DIM)
    sbo = _sb_sample(page_table, sbq.reshape(nb, dec, -1), sbk_b.reshape(nb, dec, -1), sbv_b.reshape(nb, dec, -1),
                     sb_rows(cache_sb_k), sb_rows(cache_sb_v), page, pages_step)
    lat = _mla_sample(page_table, qcat.reshape(nb, dec, -1), kcat.reshape(nb, dec, -1),
                      cache_mla_ckv.reshape(n_pool, page, KV_LORA),
                      jnp.swapaxes(cache_mla_kpe.reshape(n_pool, page, MLA_ROPE), 1, 2), pages_step)
    x1, qm = _mix_out(sbo.reshape(ms, -1), lat.reshape(ms, -1), xs, w, tms)
    n_lane_tiles = d // MEM_HEADS // LANE
    mem_rows = lambda cache: jnp.swapaxes(cache.reshape(nb, mem_t, MEM_HEADS, n_lane_tiles, LANE), 2, 3).reshape(
        nb, mem_t * MEM_HEADS * n_lane_tiles, LANE)
    attn = _mem_attn(qm.reshape(nb, dec, d), mem_rows(cache_mem_k), mem_rows(cache_mem_v), dec, tiled_rows=True)
    y_sample = _tail(x1, attn.reshape(ms, d), w, _row_tile(ms, 512), tf).reshape(nb, dec, d)
    sample_caches = (sbk.reshape(1, nb, dec, SB_KV_HEADS, HEAD_DIM), sbv.reshape(1, nb, dec, SB_KV_HEADS, HEAD_DIM),
                     ckv.reshape(1, nb, dec, KV_LORA), kpe.reshape(1, nb, dec, MLA_ROPE))

    return (y_prompt, y_sample) + prompt_caches + sample_caches
```

```python
import functools

import jax
import jax.numpy as jnp
from jax import lax
from jax.experimental import pallas as pl
from jax.experimental.pallas import tpu as pltpu

F32 = jnp.float32
BF16 = jnp.bfloat16

EPS = 1e-6
NEG_INF = -1e30
ROPE_BASE = 10000.0
HEAD_DIM = 128
SB_KV_HEADS = 2
SB_GROUP = 4
SB_Q_W = SB_KV_HEADS * SB_GROUP * HEAD_DIM
SB_KV_W = SB_KV_HEADS * HEAD_DIM
MLA_HEADS = 8
MLA_NOPE = 128
MLA_ROPE = 64
MLA_V = 128
Q_LORA = 768
KV_LORA = 512
MEM_HEADS = 4
QK_W = KV_LORA + 128
SB_SCALE = HEAD_DIM ** -0.5
MLA_SCALE = (MLA_NOPE + MLA_ROPE) ** -0.5
LANE = 128
SB_CHAIN_PAGES = 8
MLA_CHAIN_PAGES = 8

VMEM_LIMIT_BYTES = 56 * 1024 * 1024


def _params(n_grid_dims):
    return pltpu.CompilerParams(dimension_semantics=("arbitrary",) * n_grid_dims,
                                vmem_limit_bytes=VMEM_LIMIT_BYTES)


def _resident(shape):
    nd = len(shape)
    return pl.BlockSpec(shape, lambda *_: (0,) * nd, pipeline_mode=pl.Buffered(1))


def _rms(x, g):
    return x * lax.rsqrt(jnp.mean(x * x, axis=-1, keepdims=True) + EPS) * g


def _dot(a, b):
    return jnp.dot(a, b, preferred_element_type=F32)


def _dot_nt(a, b):
    return lax.dot_general(a, b, (((1,), (1,)), ((), ())), preferred_element_type=F32)


def _softplus(z):
    return jnp.maximum(z, 0.0) + jnp.log(1.0 + jnp.exp(-jnp.abs(z)))


def _split_bf16(x):
    hi = x.astype(BF16)
    lo = (x - hi.astype(F32)).astype(BF16)
    return hi, lo


def _proj_kernel(x_ref, g_ref, wmain_ref, wkpe_ref, qan_ref, wqup_ref, kvan_ref, wuk_ref,
                 cos_ref, sina_ref, sinb_ref,
                 sbq_ref, sbk_ref, sbv_ref, sbkb_ref, sbvb_ref, ckv_ref, kpe_ref, kcat_ref, qcat_ref):
    h = _rms(x_ref[...], g_ref[...]).astype(BF16)
    p = _dot(h, wmain_ref[...])
    sbq_ref[...] = (p[:, :SB_Q_W] * SB_SCALE).astype(BF16)
    k = p[:, SB_Q_W:SB_Q_W + SB_KV_W]
    v = p[:, SB_Q_W + SB_KV_W:SB_Q_W + 2 * SB_KV_W]
    sbk_ref[...] = k
    sbv_ref[...] = v
    sbkb_ref[...] = k.astype(BF16)
    sbvb_ref[...] = v.astype(BF16)
    o_qc = SB_Q_W + 2 * SB_KV_W
    qc = p[:, o_qc:o_qc + Q_LORA]
    ckv = _rms(p[:, o_qc + Q_LORA:o_qc + Q_LORA + KV_LORA], kvan_ref[...])
    ckv_ref[...] = ckv

    cos = cos_ref[...]
    sina = sina_ref[...]
    sinb = sinb_ref[...]

    def rope(t):
        return t * cos + pltpu.roll(t, LANE - MLA_ROPE // 2, 1) * sina + pltpu.roll(t, MLA_ROPE // 2, 1) * sinb

    kp = rope(_dot(h, wkpe_ref[...]))
    kpe_ref[...] = kp[:, :MLA_ROPE]
    kcat_ref[:, :KV_LORA] = ckv.astype(BF16)
    kcat_ref[:, KV_LORA:] = kp.astype(BF16)

    qn = _rms(qc, qan_ref[...]).astype(BF16)
    q = _dot(qn, wqup_ref[...])
    for hd in range(MLA_HEADS):
        q_nope = q[:, hd * MLA_NOPE:(hd + 1) * MLA_NOPE].astype(BF16)
        q_lat = _dot(q_nope, wuk_ref[hd]) * MLA_SCALE
        qcat_ref[:, hd * QK_W:hd * QK_W + KV_LORA] = q_lat.astype(BF16)
        o = MLA_HEADS * MLA_NOPE + hd * LANE
        q_pe = rope(q[:, o:o + LANE]) * MLA_SCALE
        qcat_ref[:, hd * QK_W + KV_LORA:(hd + 1) * QK_W] = q_pe.astype(BF16)


def _proj(x, pos, w, tm):
    m, d = x.shape
    n_pos_blocks = pos.shape[0] // tm
    half = MLA_ROPE // 2
    inv = ROPE_BASE ** (-jnp.arange(half, dtype=F32) / half)
    ang = pos.astype(F32)[:, None] * inv[None, :]
    cos, sin, zero = jnp.cos(ang), jnp.sin(ang), jnp.zeros_like(ang)
    cos_t = jnp.concatenate([cos, cos, zero, zero], axis=-1)
    sina_t = jnp.concatenate([-sin, zero, zero, zero], axis=-1)
    sinb_t = jnp.concatenate([zero, sin, zero, zero], axis=-1)

    row = lambda width: pl.BlockSpec((tm, width), lambda i: (i, 0))
    tab = pl.BlockSpec((tm, LANE), lambda i: (i % n_pos_blocks, 0))
    out_shapes = (
        jax.ShapeDtypeStruct((m, SB_Q_W), BF16),
        jax.ShapeDtypeStruct((m, SB_KV_W), F32),
        jax.ShapeDtypeStruct((m, SB_KV_W), F32),
        jax.ShapeDtypeStruct((m, SB_KV_W), BF16),
        jax.ShapeDtypeStruct((m, SB_KV_W), BF16),
        jax.ShapeDtypeStruct((m, KV_LORA), F32),
        jax.ShapeDtypeStruct((m, MLA_ROPE), F32),
        jax.ShapeDtypeStruct((m, QK_W), BF16),
        jax.ShapeDtypeStruct((m, MLA_HEADS * QK_W), BF16),
    )
    return pl.pallas_call(
        _proj_kernel,
        grid=(m // tm,),
        in_specs=[row(d), _resident((1, d)), _resident(w["w_main"].shape), _resident(w["w_kpe"].shape),
                  _resident((1, Q_LORA)), _resident(w["w_q_up"].shape), _resident((1, KV_LORA)),
                  _resident(w["w_uk"].shape), tab, tab, tab],
        out_specs=[row(s.shape[1]) for s in out_shapes],
        out_shape=out_shapes,
        compiler_params=_params(1),
        name="mixer_proj",
    )(x, w["norm_mix_pre"], w["w_main"], w["w_kpe"], w["q_a_norm"], w["w_q_up"], w["kv_a_norm"],
      w["w_uk"], cos_t, sina_t, sinb_t)


def _sb_scores(z, mask, tri):
    sp = _softplus(z)
    lom = -sp if mask is None else jnp.where(mask, -sp, 0.0)
    hi, lo = _split_bf16(lom)
    n = z.shape[0]
    suffix = _dot(jnp.concatenate([hi, lo], axis=0), tri)
    suffix = suffix[:n] + suffix[n:]
    return (z - sp) + suffix, suffix[:, :1] + lom[:, :1]


def _sb_prompt_kernel(q_ref, k_ref, v_ref, tri_ref, o_ref, acc_ref, run_ref, *, blk, n_blk):
    i = pl.program_id(2)
    j = pl.program_id(3)

    @pl.when(j == 0)
    def _():
        acc_ref[...] = jnp.zeros_like(acc_ref)
        run_ref[...] = jnp.zeros_like(run_ref)

    @pl.when(j <= i)
    def _():
        rows = SB_GROUP * blk
        q = jnp.concatenate([q_ref[:, g * HEAD_DIM:(g + 1) * HEAD_DIM] for g in range(SB_GROUP)], axis=0)
        z = _dot_nt(q, k_ref[...])
        t_pos = i * blk + lax.broadcasted_iota(jnp.int32, (rows, blk), 0) % blk
        s_pos = (i - j) * blk + lax.broadcasted_iota(jnp.int32, (rows, blk), 1)
        mask = s_pos < t_pos
        logit, total = _sb_scores(z, mask, tri_ref[...])
        a = jnp.where(mask, jnp.exp(logit + run_ref[...]), 0.0)
        acc_ref[...] += _dot(a.astype(BF16), v_ref[...])
        run_ref[...] += total

    @pl.when(j == n_blk - 1)
    def _():
        for g in range(SB_GROUP):
            o_ref[:, g * HEAD_DIM:(g + 1) * HEAD_DIM] = acc_ref[g * blk:(g + 1) * blk, :].astype(o_ref.dtype)


def _tri(n):
    r = lax.broadcasted_iota(jnp.int32, (n, n), 0)
    c = lax.broadcasted_iota(jnp.int32, (n, n), 1)
    return (r > c).astype(BF16)


def _sb_prompt(sbq, sbk, sbv, batch, seq, blk):
    n_blk = seq // blk
    kv_map = lambda b, n, i, j: (b * n_blk + jnp.maximum(i - j, 0), n)
    q_map = lambda b, n, i, j: (b * n_blk + i, n)
    return pl.pallas_call(
        functools.partial(_sb_prompt_kernel, blk=blk, n_blk=n_blk),
        grid=(batch, SB_KV_HEADS, n_blk, n_blk),
        in_specs=[pl.BlockSpec((blk, SB_GROUP * HEAD_DIM), q_map),
                  pl.BlockSpec((blk, HEAD_DIM), kv_map),
                  pl.BlockSpec((blk, HEAD_DIM), kv_map),
                  _resident((blk, blk))],
        out_specs=pl.BlockSpec((blk, SB_GROUP * HEAD_DIM), q_map),
        out_shape=jax.ShapeDtypeStruct((batch * seq, SB_Q_W), BF16),
        scratch_shapes=[pltpu.VMEM((SB_GROUP * blk, HEAD_DIM), F32),
                        pltpu.VMEM((SB_GROUP * blk, 1), F32)],
        compiler_params=_params(4),
        name="sb_prompt",
    )(sbq, sbk, sbv, _tri(blk))


def _sb_sample_kernel(pt_ref, q_ref, knew_ref, vnew_ref, tri_ref, k_hbm, v_hbm, o_ref, acc_ref, run_ref,
                      k_buf, v_buf, sem, *, n_pages_step, n_steps, page, dec):
    rows_pg = SB_KV_HEADS * page
    page_src = lambda pg: pl.ds(pl.multiple_of(pg * rows_pg, rows_pg), rows_pg)
    slot, drain = _fetch_pages(pt_ref, (k_hbm, v_hbm), (k_buf, v_buf), sem, n_pages_step, n_steps, page_src)
    k_refs = [k_buf.at[slot, p] for p in range(n_pages_step)]
    v_refs = [v_buf.at[slot, p] for p in range(n_pages_step)]
    c = pl.program_id(1)
    rows = SB_GROUP * dec
    qf = q_ref[...].astype(F32)
    q = [jnp.concatenate([qf[:, (n * SB_GROUP + g) * HEAD_DIM:(n * SB_GROUP + g + 1) * HEAD_DIM]
                          for g in range(SB_GROUP)], axis=0).astype(BF16) for n in range(SB_KV_HEADS)]
    head = lambda t, n: t[:, n * HEAD_DIM:(n + 1) * HEAD_DIM]

    @pl.when(c == 0)
    def _():
        kn = knew_ref[...]
        vn = vnew_ref[...]
        z = jnp.concatenate([_dot_nt(q[n], head(kn, n)) for n in range(SB_KV_HEADS)], axis=0)
        nr = SB_KV_HEADS * rows
        t_loc = lax.broadcasted_iota(jnp.int32, (nr, page), 0) % dec
        s_loc = lax.broadcasted_iota(jnp.int32, (nr, page), 1)
        mask = s_loc < t_loc
        logit, total = _sb_scores(z, mask, tri_ref[:page, :page])
        a = jnp.where(mask, jnp.exp(logit), 0.0).astype(BF16)
        for n in range(SB_KV_HEADS):
            acc_ref[n] = _dot(a[n * rows:(n + 1) * rows], head(vn, n))
            run_ref[n] = total[n * rows:(n + 1) * rows]

    cached = lambda r, n: r[pl.ds(n, page, stride=SB_KV_HEADS), :].astype(BF16)
    n_sub = SB_CHAIN_PAGES // 2
    n_chain = n_pages_step // SB_CHAIN_PAGES
    logits, totals = [], []
    for ch in range(n_chain):
        refs = k_refs[ch * SB_CHAIN_PAGES:(ch + 1) * SB_CHAIN_PAGES]
        zs = []
        for n in range(SB_KV_HEADS):
            for s in range(n_sub):
                k2 = jnp.concatenate([cached(refs[2 * s], n), cached(refs[2 * s + 1], n)], axis=0)
                zs.append(_dot_nt(q[n], k2))
        logit, total = _sb_scores(jnp.concatenate(zs, axis=0), None, tri_ref[...])
        logits.append(logit)
        totals.append(total)
    run = [run_ref[n] for n in range(SB_KV_HEADS)]
    carries = [[None] * (SB_KV_HEADS * n_sub) for _ in range(n_chain)]
    for ch in reversed(range(n_chain)):
        for n in range(SB_KV_HEADS):
            for s in reversed(range(n_sub)):
                carries[ch][n * n_sub + s] = run[n]
                o = (n * n_sub + s) * rows
                run[n] = run[n] + totals[ch][o:o + rows]
    out = [None] * SB_KV_HEADS
    for ch in range(n_chain):
        a = jnp.exp(logits[ch] + jnp.concatenate(carries[ch], axis=0)).astype(BF16)
        refs = v_refs[ch * SB_CHAIN_PAGES:(ch + 1) * SB_CHAIN_PAGES]
        for n in range(SB_KV_HEADS):
            a_n = jnp.concatenate([a[(n * n_sub + s) * rows:(n * n_sub + s + 1) * rows] for s in range(n_sub)], axis=1)
            v_n = jnp.concatenate([cached(r, n) for r in refs], axis=0)
            o_n = _dot(a_n, v_n)
            out[n] = o_n if out[n] is None else out[n] + o_n
    for n in range(SB_KV_HEADS):
        run_ref[n] = run[n]
        acc_ref[n] += out[n]

    @pl.when(c == n_steps - 1)
    def _():
        for n in range(SB_KV_HEADS):
            for g in range(SB_GROUP):
                o = (n * SB_GROUP + g) * HEAD_DIM
                o_ref[:, o:o + HEAD_DIM] = acc_ref[n, g * dec:(g + 1) * dec, :]

    drain()


def _fetch_pages(pt_ref, hbm_refs, buf_refs, sem, n_pages_step, n_steps, page_src):
    b = pl.program_id(0)
    c = pl.program_id(1)
    n_seq = pl.num_programs(0)
    t = b * n_steps + c
    slot = lax.rem(t, 2)

    def copies(bb, cc, half):
        out = []
        for p in range(n_pages_step):
            src = page_src(pt_ref[bb, (n_steps - 1 - cc) * n_pages_step + p])
            for k, (hbm, buf) in enumerate(zip(hbm_refs, buf_refs)):
                out.append(pltpu.make_async_copy(hbm.at[src], buf.at[half, p], sem.at[k, half]))
        return out

    @pl.when(t == 0)
    def _():
        for cp in copies(0, 0, 0):
            cp.start()

    nxt = t + 1
    nxt_b = jnp.minimum(nxt // n_steps, n_seq - 1)
    nxt_c = lax.rem(nxt, n_steps)
    for cp in copies(nxt_b, nxt_c, 1 - slot):
        cp.start()
    for cp in copies(b, c, slot):
        cp.wait()

    def drain():
        @pl.when(t == n_seq * n_steps - 1)
        def _():
            for cp in copies(nxt_b, nxt_c, 1 - slot):
                cp.wait()

    return slot, drain


def _pad_new(x, page):
    return jnp.pad(x, ((0, 0), (0, page - x.shape[1]), (0, 0)))


def _sb_sample(page_table, sbq, sbk_new, sbv_new, cache_k, cache_v, page, n_pages_step):
    nb, dec, _ = sbq.shape
    n_pages = page_table.shape[1]
    n_steps = n_pages // n_pages_step
    per_seq = lambda w, rows: pl.BlockSpec((None, rows, w), lambda b, c, pt: (b, 0, 0))
    kernel = functools.partial(_sb_sample_kernel, n_pages_step=n_pages_step, n_steps=n_steps, page=page, dec=dec)
    grid_spec = pltpu.PrefetchScalarGridSpec(
        num_scalar_prefetch=1,
        grid=(nb, n_steps),
        in_specs=[per_seq(SB_Q_W, dec), per_seq(SB_KV_W, page), per_seq(SB_KV_W, page),
                  pl.BlockSpec((2 * page, 2 * page), lambda b, c, pt: (0, 0)),
                  pl.BlockSpec(memory_space=pl.ANY), pl.BlockSpec(memory_space=pl.ANY)],
        out_specs=per_seq(SB_Q_W, dec),
        scratch_shapes=[pltpu.VMEM((SB_KV_HEADS, SB_GROUP * dec, HEAD_DIM), F32),
                        pltpu.VMEM((SB_KV_HEADS, SB_GROUP * dec, 1), F32),
                        pltpu.VMEM((2, n_pages_step, SB_KV_HEADS * page, HEAD_DIM), F32),
                        pltpu.VMEM((2, n_pages_step, SB_KV_HEADS * page, HEAD_DIM), F32),
                        pltpu.SemaphoreType.DMA((2, 2))],
    )
    return pl.pallas_call(
        kernel,
        grid_spec=grid_spec,
        out_shape=jax.ShapeDtypeStruct((nb, dec, SB_Q_W), F32),
        compiler_params=_params(2),
        name="sb_sample",
    )(page_table, sbq, _pad_new(sbk_new, page), _pad_new(sbv_new, page), _tri(2 * page), cache_k, cache_v)


def _online_softmax_step(s, values, m_ref, l_ref, acc_ref):
    m_old = m_ref[...]
    m_new = jnp.maximum(m_old, jnp.max(s, axis=-1, keepdims=True))
    alpha = jnp.exp(m_old - m_new)
    p = jnp.exp(s - m_new)
    l_ref[...] = alpha * l_ref[...] + jnp.sum(p, axis=-1, keepdims=True)
    acc_ref[...] = alpha * acc_ref[...] + _dot(p.astype(BF16), values)
    m_ref[...] = m_new


def _mla_prompt_kernel(q_ref, kv_ref, o_ref, m_ref, l_ref, acc_ref, *, bq, bk, n_kb):
    i = pl.program_id(1)
    j = pl.program_id(2)
    last = ((i + 1) * bq - 1) // bk

    @pl.when(j == 0)
    def _():
        m_ref[...] = jnp.full_like(m_ref, NEG_INF)
        l_ref[...] = jnp.zeros_like(l_ref)
        acc_ref[...] = jnp.zeros_like(acc_ref)

    @pl.when(j <= last)
    def _():
        rows = MLA_HEADS * bq
        q = jnp.concatenate([q_ref[:, hd * QK_W:(hd + 1) * QK_W] for hd in range(MLA_HEADS)], axis=0)
        kv = kv_ref[...]
        s = _dot_nt(q, kv)
        t_pos = i * bq + lax.broadcasted_iota(jnp.int32, (rows, bk), 0) % bq
        s_pos = (last - j) * bk + lax.broadcasted_iota(jnp.int32, (rows, bk), 1)
        s = jnp.where(s_pos <= t_pos, s, NEG_INF)
        _online_softmax_step(s, kv[:, :KV_LORA], m_ref, l_ref, acc_ref)

    @pl.when(j == n_kb - 1)
    def _():
        out = acc_ref[...] / l_ref[...]
        for hd in range(MLA_HEADS):
            o_ref[:, hd * KV_LORA:(hd + 1) * KV_LORA] = out[hd * bq:(hd + 1) * bq].astype(o_ref.dtype)


def _mla_prompt(qcat, kcat, batch, seq, bq, bk):
    assert bk % bq == 0
    n_qb = seq // bq
    n_kb = seq // bk
    rows = MLA_HEADS * bq
    kv_map = lambda b, i, j: (b * n_kb + jnp.maximum(((i + 1) * bq - 1) // bk - j, 0), 0)
    return pl.pallas_call(
        functools.partial(_mla_prompt_kernel, bq=bq, bk=bk, n_kb=n_kb),
        grid=(batch, n_qb, n_kb),
        in_specs=[pl.BlockSpec((bq, MLA_HEADS * QK_W), lambda b, i, j: (b * n_qb + i, 0)),
                  pl.BlockSpec((bk, QK_W), kv_map)],
        out_specs=pl.BlockSpec((bq, MLA_HEADS * KV_LORA), lambda b, i, j: (b * n_qb + i, 0)),
        out_shape=jax.ShapeDtypeStruct((batch * seq, MLA_HEADS * KV_LORA), BF16),
        scratch_shapes=[pltpu.VMEM((rows, 1), F32), pltpu.VMEM((rows, 1), F32),
                        pltpu.VMEM((rows, KV_LORA), F32)],
        compiler_params=_params(3),
        name="mla_prompt",
    )(qcat, kcat)


def _mla_sample_kernel(pt_ref, q_ref, kvnew_ref, ckv_hbm, kpe_hbm, o_ref, m_ref, l_ref, acc_ref,
                       ckv_buf, kpe_buf, sem, *, n_pages_step, n_steps, page, dec):
    slot, drain = _fetch_pages(pt_ref, (ckv_hbm, kpe_hbm), (ckv_buf, kpe_buf), sem, n_pages_step, n_steps,
                               lambda pg: pg)
    ckv_refs = [ckv_buf.at[slot, p] for p in range(n_pages_step)]
    kpe_refs = [kpe_buf.at[slot, p] for p in range(n_pages_step)]
    c = pl.program_id(1)
    rows = MLA_HEADS * dec
    qf = q_ref[...].astype(F32)
    q = jnp.concatenate([qf[:, hd * QK_W:(hd + 1) * QK_W] for hd in range(MLA_HEADS)], axis=0).astype(BF16)

    @pl.when(c == 0)
    def _():
        kv = kvnew_ref[...]
        s = _dot_nt(q, kv)
        t_loc = lax.broadcasted_iota(jnp.int32, (rows, page), 0) % dec
        s_loc = lax.broadcasted_iota(jnp.int32, (rows, page), 1)
        s = jnp.where(s_loc <= t_loc, s, NEG_INF)
        m = jnp.max(s, axis=-1, keepdims=True)
        p = jnp.exp(s - m)
        m_ref[...] = m
        l_ref[...] = jnp.sum(p, axis=-1, keepdims=True)
        acc_ref[...] = _dot(p.astype(BF16), kv[:, :KV_LORA])

    q_lat = q[:, :KV_LORA]
    q_pe = q[:, KV_LORA:KV_LORA + MLA_ROPE]
    parts = []
    for ch in range(n_pages_step // MLA_CHAIN_PAGES):
        sl = slice(ch * MLA_CHAIN_PAGES, (ch + 1) * MLA_CHAIN_PAGES)
        ckv = jnp.concatenate([r[...].astype(BF16) for r in ckv_refs[sl]], axis=0)
        kpe_t = jnp.concatenate([r[...].astype(BF16) for r in kpe_refs[sl]], axis=1)
        s = _dot_nt(q_lat, ckv) + _dot(q_pe, kpe_t)
        m_c = jnp.max(s, axis=-1, keepdims=True)
        p = jnp.exp(s - m_c)
        parts.append((m_c, jnp.sum(p, axis=-1, keepdims=True), _dot(p.astype(BF16), ckv)))
    m_old = m_ref[...]
    m_new = functools.reduce(jnp.maximum, [m_c for m_c, _, _ in parts], m_old)
    alpha = jnp.exp(m_old - m_new)
    l_new = alpha * l_ref[...]
    acc = alpha * acc_ref[...]
    for m_c, l_c, o_c in parts:
        w_c = jnp.exp(m_c - m_new)
        l_new = l_new + w_c * l_c
        acc = acc + w_c * o_c
    m_ref[...] = m_new
    l_ref[...] = l_new
    acc_ref[...] = acc

    @pl.when(c == n_steps - 1)
    def _():
        out = acc_ref[...] / l_ref[...]
        for hd in range(MLA_HEADS):
            o_ref[:, hd * KV_LORA:(hd + 1) * KV_LORA] = out[hd * dec:(hd + 1) * dec]

    drain()


def _mla_sample(page_table, qcat, kcat_new, cache_ckv, cache_kpe_t, n_pages_step):
    nb, dec, _ = qcat.shape
    n_pages = page_table.shape[1]
    page = cache_ckv.shape[1]
    n_steps = n_pages // n_pages_step
    rows = MLA_HEADS * dec
    per_seq = lambda w, r: pl.BlockSpec((None, r, w), lambda b, c, pt: (b, 0, 0))
    kernel = functools.partial(_mla_sample_kernel, n_pages_step=n_pages_step, n_steps=n_steps, page=page, dec=dec)
    grid_spec = pltpu.PrefetchScalarGridSpec(
        num_scalar_prefetch=1,
        grid=(nb, n_steps),
        in_specs=[per_seq(MLA_HEADS * QK_W, dec), per_seq(QK_W, page),
                  pl.BlockSpec(memory_space=pl.ANY), pl.BlockSpec(memory_space=pl.ANY)],
        out_specs=per_seq(MLA_HEADS * KV_LORA, dec),
        scratch_shapes=[pltpu.VMEM((rows, 1), F32), pltpu.VMEM((rows, 1), F32),
                        pltpu.VMEM((rows, KV_LORA), F32),
                        pltpu.VMEM((2, n_pages_step, page, KV_LORA), F32),
                        pltpu.VMEM((2, n_pages_step, MLA_ROPE, page), F32),
                        pltpu.SemaphoreType.DMA((2, 2))],
    )
    return pl.pallas_call(
        kernel,
        grid_spec=grid_spec,
        out_shape=jax.ShapeDtypeStruct((nb, dec, MLA_HEADS * KV_LORA), F32),
        compiler_params=_params(2),
        name="mla_sample",
    )(page_table, qcat, _pad_new(kcat_new, page), cache_ckv, cache_kpe_t)


def _mix_out_kernel(sbo_ref, lat_ref, x_ref, wuv_ref, wout_ref, gpost_ref, gxpre_ref, wxq_ref,
                    x1_ref, qm_ref, *, mem_scale):
    lat = lat_ref[...].astype(BF16)
    mla_o = jnp.concatenate(
        [_dot(lat[:, hd * KV_LORA:(hd + 1) * KV_LORA], wuv_ref[hd]) for hd in range(MLA_HEADS)], axis=1)
    y = _dot(sbo_ref[...].astype(BF16), wout_ref[:SB_Q_W, :]) + _dot(mla_o.astype(BF16), wout_ref[SB_Q_W:, :])
    x1 = x_ref[...] + _rms(y, gpost_ref[...])
    x1_ref[...] = x1
    qm = _dot(_rms(x1, gxpre_ref[...]).astype(BF16), wxq_ref[...]) * mem_scale
    qm_ref[...] = qm.astype(qm_ref.dtype)


def _mix_out(sbo, lat, x, w, tm):
    m, d = x.shape
    row = lambda width: pl.BlockSpec((tm, width), lambda i: (i, 0))
    mem_scale = (d // MEM_HEADS) ** -0.5
    return pl.pallas_call(
        functools.partial(_mix_out_kernel, mem_scale=mem_scale),
        grid=(m // tm,),
        in_specs=[row(sbo.shape[1]), row(lat.shape[1]), row(d), _resident(w["w_uv"].shape),
                  _resident(w["w_out"].shape), _resident((1, d)), _resident((1, d)), _resident(w["w_xq"].shape)],
        out_specs=[row(d), row(d)],
        out_shape=(jax.ShapeDtypeStruct((m, d), F32), jax.ShapeDtypeStruct((m, d), BF16)),
        compiler_params=_params(1),
        name="mixer_out",
    )(sbo, lat, x, w["w_uv"], w["w_out"], w["norm_mix_post"], w["norm_x_pre"], w["w_xq"])


def _norm_matmul_kernel(x_ref, g_ref, w_ref, o_ref, ob_ref, h_ref):
    @pl.when(pl.program_id(1) == 0)
    def _():
        h_ref[...] = _rms(x_ref[...], g_ref[...]).astype(BF16)

    y = _dot(h_ref[...], w_ref[...])
    o_ref[...] = y
    ob_ref[...] = y.astype(BF16)


def _norm_matmul(x, g, w, tm, tn):
    m, d = x.shape
    n = w.shape[1]
    return pl.pallas_call(
        _norm_matmul_kernel,
        grid=(m // tm, n // tn),
        in_specs=[pl.BlockSpec((tm, d), lambda i, j: (i, 0)), pl.BlockSpec((1, d), lambda i, j: (0, 0)),
                  pl.BlockSpec((d, tn), lambda i, j: (0, j))],
        out_specs=[pl.BlockSpec((tm, tn), lambda i, j: (i, j))] * 2,
        out_shape=(jax.ShapeDtypeStruct((m, n), F32), jax.ShapeDtypeStruct((m, n), BF16)),
        scratch_shapes=[pltpu.VMEM((tm, d), BF16)],
        compiler_params=_params(2),
        name="memory_kv",
    )(x, g, w)


def _mem_attn_kernel(q_ref, k_ref, v_ref, o_ref, *, tiled_rows):
    hd_w = q_ref.shape[-1] // MEM_HEADS
    n_lane_tiles = hd_w // LANE
    q = q_ref[...]

    def head_of(ref, hd):
        if not tiled_rows:
            return ref[:, hd * hd_w:(hd + 1) * hd_w].astype(BF16)
        stride = n_lane_tiles * MEM_HEADS
        mt = ref.shape[0] // stride
        return jnp.concatenate([ref[pl.ds(j * MEM_HEADS + hd, mt, stride=stride), :].astype(BF16)
                                for j in range(n_lane_tiles)], axis=1)

    for hd in range(MEM_HEADS):
        sl = slice(hd * hd_w, (hd + 1) * hd_w)
        s = _dot_nt(q[:, sl], head_of(k_ref, hd))
        p = jnp.exp(s - jnp.max(s, axis=-1, keepdims=True))
        o = _dot(p.astype(BF16), head_of(v_ref, hd)) / jnp.sum(p, axis=-1, keepdims=True)
        o_ref[:, sl] = o.astype(o_ref.dtype)


def _mem_attn(q, k, v, tq, tiled_rows=False):
    nb, t, d = q.shape
    kv_block = (None,) + k.shape[1:]
    return pl.pallas_call(
        functools.partial(_mem_attn_kernel, tiled_rows=tiled_rows),
        grid=(nb, t // tq),
        in_specs=[pl.BlockSpec((None, tq, d), lambda b, i: (b, i, 0)),
                  pl.BlockSpec(kv_block, lambda b, i: (b, 0, 0)),
                  pl.BlockSpec(kv_block, lambda b, i: (b, 0, 0))],
        out_specs=pl.BlockSpec((None, tq, d), lambda b, i: (b, i, 0)),
        out_shape=jax.ShapeDtypeStruct((nb, t, d), q.dtype),
        compiler_params=_params(2),
        name="memory_attn",
    )(q, k, v)


def _out_proj_kernel(a_ref, x_ref, w_ref, g_ref, o_ref):
    o_ref[...] = x_ref[...] + _rms(_dot(a_ref[...].astype(BF16), w_ref[...]), g_ref[...])


def _out_proj(a, x, w, g, tm):
    m, d = x.shape
    row = lambda width: pl.BlockSpec((tm, width), lambda i: (i, 0))
    return pl.pallas_call(
        _out_proj_kernel,
        grid=(m // tm,),
        in_specs=[row(a.shape[1]), row(d), _resident(w.shape), _resident((1, d))],
        out_specs=row(d),
        out_shape=jax.ShapeDtypeStruct((m, d), F32),
        compiler_params=_params(1),
        name="cross_out",
    )(a, x, w, g)


def _mlp_kernel(x_ref, gpre_ref, wup_ref, wdown_ref, gpost_ref, o_ref, h_ref, acc_ref):
    f = pl.program_id(1)

    @pl.when(f == 0)
    def _():
        h_ref[...] = _rms(x_ref[...], gpre_ref[...]).astype(BF16)
        acc_ref[...] = jnp.zeros_like(acc_ref)

    u = jnp.maximum(_dot(h_ref[...], wup_ref[...]), 0.0)
    acc_ref[...] += _dot((u * u).astype(BF16), wdown_ref[...])

    @pl.when(f == pl.num_programs(1) - 1)
    def _():
        o_ref[...] = x_ref[...] + _rms(acc_ref[...], gpost_ref[...])


def _mlp(x, gpre, w_up, w_down, gpost, tm, tf):
    m, d = x.shape
    dff = w_up.shape[1]
    return pl.pallas_call(
        _mlp_kernel,
        grid=(m // tm, dff // tf),
        in_specs=[pl.BlockSpec((tm, d), lambda i, f: (i, 0)), pl.BlockSpec((1, d), lambda i, f: (0, 0)),
                  pl.BlockSpec((d, tf), lambda i, f: (0, f)), pl.BlockSpec((tf, d), lambda i, f: (f, 0)),
                  pl.BlockSpec((1, d), lambda i, f: (0, 0))],
        out_specs=pl.BlockSpec((tm, d), lambda i, f: (i, 0)),
        out_shape=jax.ShapeDtypeStruct((m, d), F32),
        scratch_shapes=[pltpu.VMEM((tm, d), BF16), pltpu.VMEM((tm, d), F32)],
        compiler_params=_params(2),
        name="mlp",
    )(x, gpre, w_up, w_down, gpost)


def _prepare_weights(norm_mix_pre, norm_mix_post, w_in, q_a_norm, w_q_up, kv_a_norm, w_kv_up, w_out, norm_mem,
                     norm_x_pre, norm_x_post, w_xq, w_xk, w_xv, w_xo, norm_ffn_pre, norm_ffn_post, w_up, w_down):
    d = w_in.shape[0]
    n_main = SB_Q_W + 2 * SB_KV_W + Q_LORA + KV_LORA
    wq = w_q_up.reshape(Q_LORA, MLA_HEADS, MLA_NOPE + MLA_ROPE)
    wq_nope = wq[:, :, :MLA_NOPE].reshape(Q_LORA, MLA_HEADS * MLA_NOPE)
    wq_rope = jnp.pad(wq[:, :, MLA_NOPE:], ((0, 0), (0, 0), (0, LANE - MLA_ROPE))).reshape(Q_LORA, MLA_HEADS * LANE)
    w_kv = w_kv_up.reshape(KV_LORA, MLA_HEADS, MLA_NOPE + MLA_V)
    vec = lambda g: g.reshape(1, -1).astype(F32)
    return {
        "norm_mix_pre": vec(norm_mix_pre), "norm_mix_post": vec(norm_mix_post), "q_a_norm": vec(q_a_norm),
        "kv_a_norm": vec(kv_a_norm), "norm_mem": vec(norm_mem), "norm_x_pre": vec(norm_x_pre),
        "norm_x_post": vec(norm_x_post), "norm_ffn_pre": vec(norm_ffn_pre), "norm_ffn_post": vec(norm_ffn_post),
        "w_main": w_in[:, :n_main].astype(BF16),
        "w_kpe": jnp.pad(w_in[:, n_main:], ((0, 0), (0, LANE - MLA_ROPE))).astype(BF16),
        "w_q_up": jnp.concatenate([wq_nope, wq_rope], axis=1).astype(BF16),
        "w_uk": jnp.transpose(w_kv[:, :, :MLA_NOPE], (1, 2, 0)).astype(BF16),
        "w_uv": jnp.transpose(w_kv[:, :, MLA_NOPE:], (1, 0, 2)).astype(BF16),
        "w_out": w_out.astype(BF16), "w_xq": w_xq.astype(BF16), "w_xk": w_xk.astype(BF16),
        "w_xv": w_xv.astype(BF16), "w_xo": w_xo.astype(BF16), "w_up": w_up.astype(BF16),
        "w_down": w_down.astype(BF16),
    }


def _row_tile(m, cap):
    t = min(m, cap)
    while m % t:
        t //= 2
    return t


def _tail(x1, attn, w, tm, tf):
    x2 = _out_proj(attn, x1, w["w_xo"], w["norm_x_post"], tm)
    return _mlp(x2, w["norm_ffn_pre"], w["w_up"], w["w_down"], w["norm_ffn_post"], tm, tf)


def kernel(x_prompt, x_sample, mem_prompt, cache_sb_k, cache_sb_v, cache_mla_ckv, cache_mla_kpe, cache_mem_k, cache_mem_v, page_table, norm_mix_pre, norm_mix_post, w_in, q_a_norm, w_q_up, kv_a_norm, w_kv_up, w_out, norm_mem, norm_x_pre, norm_x_post, w_xq, w_xk, w_xv, w_xo, norm_ffn_pre, norm_ffn_post, w_up, w_down):
    depth = w_in.shape[0]
    assert depth == 1, "single-layer step"
    batch, seq, d = x_prompt.shape
    nb, dec, _ = x_sample.shape
    mem_t = mem_prompt.shape[1]
    page = cache_sb_k.shape[2]
    n_pages = page_table.shape[1]
    past_len = n_pages * page
    w = _prepare_weights(*(a[0] for a in (norm_mix_pre, norm_mix_post, w_in, q_a_norm, w_q_up, kv_a_norm, w_kv_up,
                                          w_out, norm_mem, norm_x_pre, norm_x_post, w_xq, w_xk, w_xv, w_xo,
                                          norm_ffn_pre, norm_ffn_post, w_up, w_down)))
    blk = _row_tile(seq, 256)
    tf = _row_tile(w_up.shape[2], 1024)
    pages_step = _row_tile(n_pages, 32)

    ms = nb * dec
    tms = _row_tile(ms, 256)
    xs = x_sample.reshape(ms, d)
    pos_s = past_len + jnp.arange(dec, dtype=jnp.int32)
    pos_tab = jnp.tile(pos_s, tms // dec)
    (sbq, sbk, sbv, sbk_b, sbv_b, ckv, kpe, kcat, qcat) = _proj(xs, pos_tab, w, tms)
    n_pool = cache_sb_k.shape[1]
    sb_rows = lambda cache: cache.reshape(n_pool * page * SB_KV_HEADS, HEAD_DIM)
    sbo = _sb_sample(page_table, sbq.reshape(nb, dec, -1), sbk_b.reshape(nb, dec, -1), sbv_b.reshape(nb, dec, -1),
                     sb_rows(cache_sb_k), sb_rows(cache_sb_v), page, pages_step)
    lat = _mla_sample(page_table, qcat.reshape(nb, dec, -1), kcat.reshape(nb, dec, -1),
                      cache_mla_ckv.reshape(n_pool, page, KV_LORA),
                      jnp.swapaxes(cache_mla_kpe.reshape(n_pool, page, MLA_ROPE), 1, 2), pages_step)
    x1, qm = _mix_out(sbo.reshape(ms, -1), lat.reshape(ms, -1), xs, w, tms)
    n_lane_tiles = d // MEM_HEADS // LANE
    mem_rows = lambda cache: jnp.swapaxes(cache.reshape(nb, mem_t, MEM_HEADS, n_lane_tiles, LANE), 2, 3).reshape(
        nb, mem_t * MEM_HEADS * n_lane_tiles, LANE)
    attn = _mem_attn(qm.reshape(nb, dec, d), mem_rows(cache_mem_k), mem_rows(cache_mem_v), dec, tiled_rows=True)
    y_sample = _tail(x1, attn.reshape(ms, d), w, _row_tile(ms, 512), tf).reshape(nb, dec, d)
    sample_caches = (sbk.reshape(1, nb, dec, SB_KV_HEADS, HEAD_DIM), sbv.reshape(1, nb, dec, SB_KV_HEADS, HEAD_DIM),
                     ckv.reshape(1, nb, dec, KV_LORA), kpe.reshape(1, nb, dec, MLA_ROPE))

    mp = batch * seq
    tm = _row_tile(mp, 256)
    xp = x_prompt.reshape(mp, d)
    (sbq, sbk, sbv, sbk_b, sbv_b, ckv, kpe, kcat, qcat) = _proj(xp, jnp.arange(seq, dtype=jnp.int32), w, tm)
    sbo = _sb_prompt(sbq, sbk_b, sbv_b, batch, seq, blk)
    lat = _mla_prompt(qcat, kcat, batch, seq, _row_tile(seq, 128), _row_tile(seq, 512))
    x1, qm = _mix_out(sbo, lat, xp, w, tm)
    mem = mem_prompt.reshape(batch * mem_t, d)
    tmm = _row_tile(batch * mem_t, 512)
    mk, mk_b = _norm_matmul(mem, w["norm_mem"], w["w_xk"], tmm, 1024)
    mv, mv_b = _norm_matmul(mem, w["norm_mem"], w["w_xv"], tmm, 1024)
    attn = _mem_attn(qm.reshape(batch, seq, d), mk_b.reshape(batch, mem_t, d), mv_b.reshape(batch, mem_t, d), blk)
    y_prompt = _tail(x1, attn.reshape(mp, d), w, _row_tile(mp, 512), tf).reshape(batch, seq, d)
    prompt_caches = (sbk.reshape(1, batch, seq, SB_KV_HEADS, HEAD_DIM), sbv.reshape(1, batch, seq, SB_KV_HEADS, HEAD_DIM),
                     ckv.reshape(1, batch, seq, KV_LORA), kpe.reshape(1, batch, seq, MLA_ROPE),
                     mk.reshape(1, batch, mem_t, MEM_HEADS, d // MEM_HEADS),
                     mv.reshape(1, batch, mem_t, MEM_HEADS, d // MEM_HEADS))

    return (y_prompt, y_sample) + prompt_caches + sample_caches
```

```python
import functools

import jax
import jax.numpy as jnp
import numpy as np
from jax import lax
from jax.experimental import pallas as pl
from jax.experimental.pallas import tpu as pltpu

F32 = jnp.float32
BF16 = jnp.bfloat16

EPS = 1e-6
NEG_INF = -1e30
ROPE_BASE = 10000.0
HEAD_DIM = 128
SB_KV_HEADS = 2
SB_GROUP = 4
SB_Q_W = SB_KV_HEADS * SB_GROUP * HEAD_DIM
SB_KV_W = SB_KV_HEADS * HEAD_DIM
MLA_HEADS = 8
MLA_NOPE = 128
MLA_ROPE = 64
MLA_V = 128
Q_LORA = 768
KV_LORA = 512
MEM_HEADS = 4
QK_W = KV_LORA + 128
SB_SCALE = HEAD_DIM ** -0.5
MLA_SCALE = (MLA_NOPE + MLA_ROPE) ** -0.5
LANE = 128
SB_CHAIN_PAGES = 8
MLA_CHAIN_PAGES = 8
MLA_PROMPT_CHAINS = 2

VMEM_LIMIT_BYTES = 56 * 1024 * 1024


def _params(n_grid_dims):
    return pltpu.CompilerParams(dimension_semantics=("arbitrary",) * n_grid_dims,
                                vmem_limit_bytes=VMEM_LIMIT_BYTES)


def _resident(shape):
    nd = len(shape)
    return pl.BlockSpec(shape, lambda *_: (0,) * nd, pipeline_mode=pl.Buffered(1))


def _rms(x, g):
    return x * lax.rsqrt(jnp.mean(x * x, axis=-1, keepdims=True) + EPS) * g


def _dot(a, b):
    return jnp.dot(a, b, preferred_element_type=F32)


def _dot_nt(a, b):
    return lax.dot_general(a, b, (((1,), (1,)), ((), ())), preferred_element_type=F32)


def _softplus(z):
    return jnp.maximum(z, 0.0) + jnp.log(1.0 + jnp.exp(-jnp.abs(z)))


def _split_bf16(x):
    hi = x.astype(BF16)
    lo = (x - hi.astype(F32)).astype(BF16)
    return hi, lo


def _proj_kernel(x_ref, g_ref, wmain_ref, wkpe_ref, qan_ref, wqup_ref, kvan_ref, wuk_ref,
                 cos_ref, sina_ref, sinb_ref,
                 sbq_ref, sbk_ref, sbv_ref, sbkb_ref, sbvb_ref, ckv_ref, kpe_ref, kcat_ref, qcat_ref):
    h = _rms(x_ref[...], g_ref[...]).astype(BF16)
    p = _dot(h, wmain_ref[...])
    sbq_ref[...] = (p[:, :SB_Q_W] * SB_SCALE).astype(BF16)
    k = p[:, SB_Q_W:SB_Q_W + SB_KV_W]
    v = p[:, SB_Q_W + SB_KV_W:SB_Q_W + 2 * SB_KV_W]
    sbk_ref[...] = k
    sbv_ref[...] = v
    sbkb_ref[...] = k.astype(BF16)
    sbvb_ref[...] = v.astype(BF16)
    o_qc = SB_Q_W + 2 * SB_KV_W
    qc = p[:, o_qc:o_qc + Q_LORA]
    ckv = _rms(p[:, o_qc + Q_LORA:o_qc + Q_LORA + KV_LORA], kvan_ref[...])
    ckv_ref[...] = ckv

    cos = cos_ref[...]
    sina = sina_ref[...]
    sinb = sinb_ref[...]

    def rope(t):
        return t * cos + pltpu.roll(t, LANE - MLA_ROPE // 2, 1) * sina + pltpu.roll(t, MLA_ROPE // 2, 1) * sinb

    kp = rope(_dot(h, wkpe_ref[...]))
    kpe_ref[...] = kp[:, :MLA_ROPE]
    kcat_ref[:, :KV_LORA] = ckv.astype(BF16)
    kcat_ref[:, KV_LORA:] = kp.astype(BF16)

    qn = _rms(qc, qan_ref[...]).astype(BF16)
    q = _dot(qn, wqup_ref[...])
    for hd in range(MLA_HEADS):
        q_nope = q[:, hd * MLA_NOPE:(hd + 1) * MLA_NOPE].astype(BF16)
        q_lat = _dot(q_nope, wuk_ref[hd]) * MLA_SCALE
        qcat_ref[:, hd * QK_W:hd * QK_W + KV_LORA] = q_lat.astype(BF16)
        o = MLA_HEADS * MLA_NOPE + hd * LANE
        q_pe = rope(q[:, o:o + LANE]) * MLA_SCALE
        qcat_ref[:, hd * QK_W + KV_LORA:(hd + 1) * QK_W] = q_pe.astype(BF16)


def _proj(x, pos, w, tm):
    m, d = x.shape
    n_pos_blocks = pos.shape[0] // tm
    half = MLA_ROPE // 2
    inv = ROPE_BASE ** (-jnp.arange(half, dtype=F32) / half)
    ang = pos.astype(F32)[:, None] * inv[None, :]
    cos, sin, zero = jnp.cos(ang), jnp.sin(ang), jnp.zeros_like(ang)
    cos_t = jnp.concatenate([cos, cos, zero, zero], axis=-1)
    sina_t = jnp.concatenate([-sin, zero, zero, zero], axis=-1)
    sinb_t = jnp.concatenate([zero, sin, zero, zero], axis=-1)

    row = lambda width: pl.BlockSpec((tm, width), lambda i: (i, 0))
    tab = pl.BlockSpec((tm, LANE), lambda i: (i % n_pos_blocks, 0))
    out_shapes = (
        jax.ShapeDtypeStruct((m, SB_Q_W), BF16),
        jax.ShapeDtypeStruct((m, SB_KV_W), F32),
        jax.ShapeDtypeStruct((m, SB_KV_W), F32),
        jax.ShapeDtypeStruct((m, SB_KV_W), BF16),
        jax.ShapeDtypeStruct((m, SB_KV_W), BF16),
        jax.ShapeDtypeStruct((m, KV_LORA), F32),
        jax.ShapeDtypeStruct((m, MLA_ROPE), F32),
        jax.ShapeDtypeStruct((m, QK_W), BF16),
        jax.ShapeDtypeStruct((m, MLA_HEADS * QK_W), BF16),
    )
    return pl.pallas_call(
        _proj_kernel,
        grid=(m // tm,),
        in_specs=[row(d), _resident((1, d)), _resident(w["w_main"].shape), _resident(w["w_kpe"].shape),
                  _resident((1, Q_LORA)), _resident(w["w_q_up"].shape), _resident((1, KV_LORA)),
                  _resident(w["w_uk"].shape), tab, tab, tab],
        out_specs=[row(s.shape[1]) for s in out_shapes],
        out_shape=out_shapes,
        compiler_params=_params(1),
        name="mixer_proj",
    )(x, w["norm_mix_pre"], w["w_main"], w["w_kpe"], w["q_a_norm"], w["w_q_up"], w["kv_a_norm"],
      w["w_uk"], cos_t, sina_t, sinb_t)


def _sb_scores(z, mask, tri):
    sp = _softplus(z)
    lom = -sp if mask is None else jnp.where(mask, -sp, 0.0)
    hi, lo = _split_bf16(lom)
    n = z.shape[0]
    suffix = _dot(jnp.concatenate([hi, lo], axis=0), tri)
    suffix = suffix[:n] + suffix[n:]
    return (z - sp) + suffix, suffix[:, :1] + lom[:, :1]


def _causal_steps(n_qb, newest_of):
    qi = np.concatenate([np.full(newest_of(i) + 1, i) for i in range(n_qb)])
    kj = np.concatenate([np.arange(newest_of(i) + 1) for i in range(n_qb)])
    return jnp.asarray(qi, jnp.int32), jnp.asarray(kj, jnp.int32)


def _sb_prompt_kernel(qi_ref, kj_ref, q_ref, k_ref, v_ref, tri_ref, o_ref, acc_ref, run_ref, *, blk):
    i = qi_ref[pl.program_id(2)]
    j = kj_ref[pl.program_id(2)]

    @pl.when(j == 0)
    def _():
        acc_ref[...] = jnp.zeros_like(acc_ref)
        run_ref[...] = jnp.zeros_like(run_ref)

    def key_block(diagonal):
        rows = SB_GROUP * blk
        q = jnp.concatenate([q_ref[:, g * HEAD_DIM:(g + 1) * HEAD_DIM] for g in range(SB_GROUP)], axis=0)
        z = _dot_nt(q, k_ref[...])
        mask = None
        if diagonal:
            mask = (lax.broadcasted_iota(jnp.int32, (rows, blk), 1)
                    < lax.broadcasted_iota(jnp.int32, (rows, blk), 0) % blk)
        logit, total = _sb_scores(z, mask, tri_ref[...])
        a = jnp.exp(logit + run_ref[...])
        if diagonal:
            a = jnp.where(mask, a, 0.0)
        acc_ref[...] += _dot(a.astype(BF16), v_ref[...])
        run_ref[...] += total

    pl.when(j == 0)(functools.partial(key_block, True))
    pl.when(j > 0)(functools.partial(key_block, False))

    @pl.when(j == i)
    def _():
        for g in range(SB_GROUP):
            o_ref[:, g * HEAD_DIM:(g + 1) * HEAD_DIM] = acc_ref[g * blk:(g + 1) * blk, :].astype(o_ref.dtype)


def _tri(n):
    r = lax.broadcasted_iota(jnp.int32, (n, n), 0)
    c = lax.broadcasted_iota(jnp.int32, (n, n), 1)
    return (r > c).astype(BF16)


def _sb_prompt(sbq, sbk, sbv, batch, seq, blk):
    n_blk = seq // blk
    qi, kj = _causal_steps(n_blk, lambda i: i)
    kv_map = lambda b, n, s, qi, kj: (b * n_blk + qi[s] - kj[s], n)
    q_map = lambda b, n, s, qi, kj: (b * n_blk + qi[s], n)
    grid_spec = pltpu.PrefetchScalarGridSpec(
        num_scalar_prefetch=2,
        grid=(batch, SB_KV_HEADS, qi.shape[0]),
        in_specs=[pl.BlockSpec((blk, SB_GROUP * HEAD_DIM), q_map),
                  pl.BlockSpec((blk, HEAD_DIM), kv_map),
                  pl.BlockSpec((blk, HEAD_DIM), kv_map),
                  _resident((blk, blk))],
        out_specs=pl.BlockSpec((blk, SB_GROUP * HEAD_DIM), q_map),
        scratch_shapes=[pltpu.VMEM((SB_GROUP * blk, HEAD_DIM), F32),
                        pltpu.VMEM((SB_GROUP * blk, 1), F32)],
    )
    return pl.pallas_call(
        functools.partial(_sb_prompt_kernel, blk=blk),
        grid_spec=grid_spec,
        out_shape=jax.ShapeDtypeStruct((batch * seq, SB_Q_W), BF16),
        compiler_params=_params(3),
        name="sb_prompt",
    )(qi, kj, sbq, sbk, sbv, _tri(blk))


def _sb_sample_kernel(pt_ref, q_ref, knew_ref, vnew_ref, tri_ref, k_hbm, v_hbm, o_ref, acc_ref, run_ref,
                      k_buf, v_buf, sem, *, n_pages_step, n_steps, page, dec):
    rows_pg = SB_KV_HEADS * page
    page_src = lambda pg: pl.ds(pl.multiple_of(pg * rows_pg, rows_pg), rows_pg)
    slot, drain = _fetch_pages(pt_ref, (k_hbm, v_hbm), (k_buf, v_buf), sem, n_pages_step, n_steps, page_src)
    k_refs = [k_buf.at[slot, p] for p in range(n_pages_step)]
    v_refs = [v_buf.at[slot, p] for p in range(n_pages_step)]
    c = pl.program_id(1)
    rows = SB_GROUP * dec
    qf = q_ref[...].astype(F32)
    q = [jnp.concatenate([qf[:, (n * SB_GROUP + g) * HEAD_DIM:(n * SB_GROUP + g + 1) * HEAD_DIM]
                          for g in range(SB_GROUP)], axis=0).astype(BF16) for n in range(SB_KV_HEADS)]
    head = lambda t, n: t[:, n * HEAD_DIM:(n + 1) * HEAD_DIM]

    @pl.when(c == 0)
    def _():
        kn = knew_ref[...]
        vn = vnew_ref[...]
        z = jnp.concatenate([_dot_nt(q[n], head(kn, n)) for n in range(SB_KV_HEADS)], axis=0)
        nr = SB_KV_HEADS * rows
        t_loc = lax.broadcasted_iota(jnp.int32, (nr, page), 0) % dec
        s_loc = lax.broadcasted_iota(jnp.int32, (nr, page), 1)
        mask = s_loc < t_loc
        logit, total = _sb_scores(z, mask, tri_ref[:page, :page])
        a = jnp.where(mask, jnp.exp(logit), 0.0).astype(BF16)
        for n in range(SB_KV_HEADS):
            acc_ref[n] = _dot(a[n * rows:(n + 1) * rows], head(vn, n))
            run_ref[n] = total[n * rows:(n + 1) * rows]

    cached = lambda r, n: r[pl.ds(n, page, stride=SB_KV_HEADS), :].astype(BF16)
    n_sub = SB_CHAIN_PAGES // 2
    n_chain = n_pages_step // SB_CHAIN_PAGES
    logits, totals = [], []
    for ch in range(n_chain):
        refs = k_refs[ch * SB_CHAIN_PAGES:(ch + 1) * SB_CHAIN_PAGES]
        zs = []
        for n in range(SB_KV_HEADS):
            for s in range(n_sub):
                k2 = jnp.concatenate([cached(refs[2 * s], n), cached(refs[2 * s + 1], n)], axis=0)
                zs.append(_dot_nt(q[n], k2))
        logit, total = _sb_scores(jnp.concatenate(zs, axis=0), None, tri_ref[...])
        logits.append(logit)
        totals.append(total)
    run = [run_ref[n] for n in range(SB_KV_HEADS)]
    carries = [[None] * (SB_KV_HEADS * n_sub) for _ in range(n_chain)]
    for ch in reversed(range(n_chain)):
        for n in range(SB_KV_HEADS):
            for s in reversed(range(n_sub)):
                carries[ch][n * n_sub + s] = run[n]
                o = (n * n_sub + s) * rows
                run[n] = run[n] + totals[ch][o:o + rows]
    out = [None] * SB_KV_HEADS
    for ch in range(n_chain):
        a = jnp.exp(logits[ch] + jnp.concatenate(carries[ch], axis=0)).astype(BF16)
        refs = v_refs[ch * SB_CHAIN_PAGES:(ch + 1) * SB_CHAIN_PAGES]
        for n in range(SB_KV_HEADS):
            a_n = jnp.concatenate([a[(n * n_sub + s) * rows:(n * n_sub + s + 1) * rows] for s in range(n_sub)], axis=1)
            v_n = jnp.concatenate([cached(r, n) for r in refs], axis=0)
            o_n = _dot(a_n, v_n)
            out[n] = o_n if out[n] is None else out[n] + o_n
    for n in range(SB_KV_HEADS):
        run_ref[n] = run[n]
        acc_ref[n] += out[n]

    @pl.when(c == n_steps - 1)
    def _():
        for n in range(SB_KV_HEADS):
            for g in range(SB_GROUP):
                o = (n * SB_GROUP + g) * HEAD_DIM
                o_ref[:, o:o + HEAD_DIM] = acc_ref[n, g * dec:(g + 1) * dec, :]

    drain()


def _fetch_pages(pt_ref, hbm_refs, buf_refs, sem, n_pages_step, n_steps, page_src):
    b = pl.program_id(0)
    c = pl.program_id(1)
    n_seq = pl.num_programs(0)
    t = b * n_steps + c
    slot = lax.rem(t, 2)

    def copies(bb, cc, half):
        out = []
        for p in range(n_pages_step):
            src = page_src(pt_ref[bb, (n_steps - 1 - cc) * n_pages_step + p])
            for k, (hbm, buf) in enumerate(zip(hbm_refs, buf_refs)):
                out.append(pltpu.make_async_copy(hbm.at[src], buf.at[half, p], sem.at[k, half]))
        return out

    @pl.when(t == 0)
    def _():
        for cp in copies(0, 0, 0):
            cp.start()

    nxt = t + 1
    nxt_b = jnp.minimum(nxt // n_steps, n_seq - 1)
    nxt_c = lax.rem(nxt, n_steps)
    for cp in copies(nxt_b, nxt_c, 1 - slot):
        cp.start()
    for cp in copies(b, c, slot):
        cp.wait()

    def drain():
        @pl.when(t == n_seq * n_steps - 1)
        def _():
            for cp in copies(nxt_b, nxt_c, 1 - slot):
                cp.wait()

    return slot, drain


def _pad_new(x, page):
    return jnp.pad(x, ((0, 0), (0, page - x.shape[1]), (0, 0)))


def _sb_sample(page_table, sbq, sbk_new, sbv_new, cache_k, cache_v, page, n_pages_step):
    nb, dec, _ = sbq.shape
    n_pages = page_table.shape[1]
    n_steps = n_pages // n_pages_step
    per_seq = lambda w, rows: pl.BlockSpec((None, rows, w), lambda b, c, pt: (b, 0, 0))
    kernel = functools.partial(_sb_sample_kernel, n_pages_step=n_pages_step, n_steps=n_steps, page=page, dec=dec)
    grid_spec = pltpu.PrefetchScalarGridSpec(
        num_scalar_prefetch=1,
        grid=(nb, n_steps),
        in_specs=[per_seq(SB_Q_W, dec), per_seq(SB_KV_W, page), per_seq(SB_KV_W, page),
                  pl.BlockSpec((2 * page, 2 * page), lambda b, c, pt: (0, 0)),
                  pl.BlockSpec(memory_space=pl.ANY), pl.BlockSpec(memory_space=pl.ANY)],
        out_specs=per_seq(SB_Q_W, dec),
        scratch_shapes=[pltpu.VMEM((SB_KV_HEADS, SB_GROUP * dec, HEAD_DIM), F32),
                        pltpu.VMEM((SB_KV_HEADS, SB_GROUP * dec, 1), F32),
                        pltpu.VMEM((2, n_pages_step, SB_KV_HEADS * page, HEAD_DIM), F32),
                        pltpu.VMEM((2, n_pages_step, SB_KV_HEADS * page, HEAD_DIM), F32),
                        pltpu.SemaphoreType.DMA((2, 2))],
    )
    return pl.pallas_call(
        kernel,
        grid_spec=grid_spec,
        out_shape=jax.ShapeDtypeStruct((nb, dec, SB_Q_W), F32),
        compiler_params=_params(2),
        name="sb_sample",
    )(page_table, sbq, _pad_new(sbk_new, page), _pad_new(sbv_new, page), _tri(2 * page), cache_k, cache_v)


def _lanes(x, width):
    return jnp.tile(x, (1, width // LANE))


def _online_softmax_step(s, values, m_ref, l_ref, acc_ref):
    m_old = m_ref[...]
    m_new = jnp.maximum(m_old, jnp.max(s, axis=-1, keepdims=True))
    alpha = jnp.exp(m_old - m_new)
    p = jnp.exp(s - _lanes(m_new, s.shape[1]))
    l_ref[...] = alpha * l_ref[...] + jnp.sum(p, axis=-1, keepdims=True)
    acc_ref[...] = _lanes(alpha, acc_ref.shape[1]) * acc_ref[...] + _dot(p.astype(BF16), values)
    m_ref[...] = m_new


def _mla_prompt_kernel(qi_ref, kj_ref, q_ref, kv_ref, o_ref, m_ref, l_ref, acc_ref, *, bq, bk):
    i = qi_ref[pl.program_id(1)]
    j = kj_ref[pl.program_id(1)]
    last = ((i + 1) * bq - 1) // bk

    @pl.when(j == 0)
    def _():
        m_ref[...] = jnp.full_like(m_ref, NEG_INF)
        l_ref[...] = jnp.zeros_like(l_ref)
        acc_ref[...] = jnp.zeros_like(acc_ref)

    def key_block(newest):
        heads = MLA_HEADS // MLA_PROMPT_CHAINS
        rows = heads * bq
        kv = kv_ref[...]
        if newest:
            t_pos = i * bq + lax.broadcasted_iota(jnp.int32, (rows, bk), 0) % bq
            s_pos = last * bk + lax.broadcasted_iota(jnp.int32, (rows, bk), 1)
            visible = s_pos <= t_pos
        for ch in range(MLA_PROMPT_CHAINS):
            q = jnp.concatenate([q_ref[:, hd * QK_W:(hd + 1) * QK_W]
                                 for hd in range(ch * heads, (ch + 1) * heads)], axis=0)
            s = _dot_nt(q, kv)
            if newest:
                s = jnp.where(visible, s, NEG_INF)
            sl = pl.ds(ch * rows, rows)
            _online_softmax_step(s, kv[:, :KV_LORA], m_ref.at[sl], l_ref.at[sl], acc_ref.at[sl])

    pl.when(j == 0)(functools.partial(key_block, True))
    pl.when(j > 0)(functools.partial(key_block, False))

    @pl.when(j == last)
    def _():
        out = acc_ref[...] / _lanes(l_ref[...], KV_LORA)
        for hd in range(MLA_HEADS):
            o_ref[:, hd * KV_LORA:(hd + 1) * KV_LORA] = out[hd * bq:(hd + 1) * bq].astype(o_ref.dtype)


def _mla_prompt(qcat, kcat, batch, seq, bq, bk):
    assert bk % bq == 0
    n_qb = seq // bq
    n_kb = seq // bk
    rows = MLA_HEADS * bq
    newest_of = lambda i: ((i + 1) * bq - 1) // bk
    qi, kj = _causal_steps(n_qb, newest_of)
    q_map = lambda b, s, qi, kj: (b * n_qb + qi[s], 0)
    kv_map = lambda b, s, qi, kj: (b * n_kb + newest_of(qi[s]) - kj[s], 0)
    grid_spec = pltpu.PrefetchScalarGridSpec(
        num_scalar_prefetch=2,
        grid=(batch, qi.shape[0]),
        in_specs=[pl.BlockSpec((bq, MLA_HEADS * QK_W), q_map), pl.BlockSpec((bk, QK_W), kv_map)],
        out_specs=pl.BlockSpec((bq, MLA_HEADS * KV_LORA), q_map),
        scratch_shapes=[pltpu.VMEM((rows, LANE), F32), pltpu.VMEM((rows, LANE), F32),
                        pltpu.VMEM((rows, KV_LORA), F32)],
    )
    return pl.pallas_call(
        functools.partial(_mla_prompt_kernel, bq=bq, bk=bk),
        grid_spec=grid_spec,
        out_shape=jax.ShapeDtypeStruct((batch * seq, MLA_HEADS * KV_LORA), BF16),
        compiler_params=_params(2),
        name="mla_prompt",
    )(qi, kj, qcat, kcat)


def _mla_sample_kernel(pt_ref, q_ref, kvnew_ref, ckv_hbm, kpe_hbm, o_ref, m_ref, l_ref, acc_ref,
                       ckv_buf, kpe_buf, sem, *, n_pages_step, n_steps, page, dec):
    slot, drain = _fetch_pages(pt_ref, (ckv_hbm, kpe_hbm), (ckv_buf, kpe_buf), sem, n_pages_step, n_steps,
                               lambda pg: pg)
    ckv_refs = [ckv_buf.at[slot, p] for p in range(n_pages_step)]
    kpe_refs = [kpe_buf.at[slot, p] for p in range(n_pages_step)]
    c = pl.program_id(1)
    rows = MLA_HEADS * dec
    qf = q_ref[...].astype(F32)
    q = jnp.concatenate([qf[:, hd * QK_W:(hd + 1) * QK_W] for hd in range(MLA_HEADS)], axis=0).astype(BF16)

    @pl.when(c == 0)
    def _():
        kv = kvnew_ref[...]
        s = _dot_nt(q, kv)
        t_loc = lax.broadcasted_iota(jnp.int32, (rows, page), 0) % dec
        s_loc = lax.broadcasted_iota(jnp.int32, (rows, page), 1)
        s = jnp.where(s_loc <= t_loc, s, NEG_INF)
        m = jnp.max(s, axis=-1, keepdims=True)
        p = jnp.exp(s - m)
        m_ref[...] = m
        l_ref[...] = jnp.sum(p, axis=-1, keepdims=True)
        acc_ref[...] = _dot(p.astype(BF16), kv[:, :KV_LORA])

    q_lat = q[:, :KV_LORA]
    q_pe = q[:, KV_LORA:KV_LORA + MLA_ROPE]
    parts = []
    for ch in range(n_pages_step // MLA_CHAIN_PAGES):
        sl = slice(ch * MLA_CHAIN_PAGES, (ch + 1) * MLA_CHAIN_PAGES)
        ckv = jnp.concatenate([r[...].astype(BF16) for r in ckv_refs[sl]], axis=0)
        kpe_t = jnp.concatenate([r[...].astype(BF16) for r in kpe_refs[sl]], axis=1)
        s = _dot_nt(q_lat, ckv) + _dot(q_pe, kpe_t)
        m_c = jnp.max(s, axis=-1, keepdims=True)
        p = jnp.exp(s - m_c)
        parts.append((m_c, jnp.sum(p, axis=-1, keepdims=True), _dot(p.astype(BF16), ckv)))
    m_old = m_ref[...]
    m_new = functools.reduce(jnp.maximum, [m_c for m_c, _, _ in parts], m_old)
    alpha = jnp.exp(m_old - m_new)
    l_new = alpha * l_ref[...]
    acc = alpha * acc_ref[...]
    for m_c, l_c, o_c in parts:
        w_c = jnp.exp(m_c - m_new)
        l_new = l_new + w_c * l_c
        acc = acc + w_c * o_c
    m_ref[...] = m_new
    l_ref[...] = l_new
    acc_ref[...] = acc

    @pl.when(c == n_steps - 1)
    def _():
        out = acc_ref[...] / l_ref[...]
        for hd in range(MLA_HEADS):
            o_ref[:, hd * KV_LORA:(hd + 1) * KV_LORA] = out[hd * dec:(hd + 1) * dec]

    drain()


def _mla_sample(page_table, qcat, kcat_new, cache_ckv, cache_kpe_t, n_pages_step):
    nb, dec, _ = qcat.shape
    n_pages = page_table.shape[1]
    page = cache_ckv.shape[1]
    n_steps = n_pages // n_pages_step
    rows = MLA_HEADS * dec
    per_seq = lambda w, r: pl.BlockSpec((None, r, w), lambda b, c, pt: (b, 0, 0))
    kernel = functools.partial(_mla_sample_kernel, n_pages_step=n_pages_step, n_steps=n_steps, page=page, dec=dec)
    grid_spec = pltpu.PrefetchScalarGridSpec(
        num_scalar_prefetch=1,
        grid=(nb, n_steps),
        in_specs=[per_seq(MLA_HEADS * QK_W, dec), per_seq(QK_W, page),
                  pl.BlockSpec(memory_space=pl.ANY), pl.BlockSpec(memory_space=pl.ANY)],
        out_specs=per_seq(MLA_HEADS * KV_LORA, dec),
        scratch_shapes=[pltpu.VMEM((rows, 1), F32), pltpu.VMEM((rows, 1), F32),
                        pltpu.VMEM((rows, KV_LORA), F32),
                        pltpu.VMEM((2, n_pages_step, page, KV_LORA), F32),
                        pltpu.VMEM((2, n_pages_step, MLA_ROPE, page), F32),
                        pltpu.SemaphoreType.DMA((2, 2))],
    )
    return pl.pallas_call(
        kernel,
        grid_spec=grid_spec,
        out_shape=jax.ShapeDtypeStruct((nb, dec, MLA_HEADS * KV_LORA), F32),
        compiler_params=_params(2),
        name="mla_sample",
    )(page_table, qcat, _pad_new(kcat_new, page), cache_ckv, cache_kpe_t)


def _mix_out_kernel(sbo_ref, lat_ref, x_ref, wuv_ref, wout_ref, gpost_ref, gxpre_ref, wxq_ref,
                    x1_ref, qm_ref, *, mem_scale):
    lat = lat_ref[...].astype(BF16)
    mla_o = jnp.concatenate(
        [_dot(lat[:, hd * KV_LORA:(hd + 1) * KV_LORA], wuv_ref[hd]) for hd in range(MLA_HEADS)], axis=1)
    y = _dot(sbo_ref[...].astype(BF16), wout_ref[:SB_Q_W, :]) + _dot(mla_o.astype(BF16), wout_ref[SB_Q_W:, :])
    x1 = x_ref[...] + _rms(y, gpost_ref[...])
    x1_ref[...] = x1
    qm = _dot(_rms(x1, gxpre_ref[...]).astype(BF16), wxq_ref[...]) * mem_scale
    qm_ref[...] = qm.astype(qm_ref.dtype)


def _mix_out(sbo, lat, x, w, tm):
    m, d = x.shape
    row = lambda width: pl.BlockSpec((tm, width), lambda i: (i, 0))
    mem_scale = (d // MEM_HEADS) ** -0.5
    return pl.pallas_call(
        functools.partial(_mix_out_kernel, mem_scale=mem_scale),
        grid=(m // tm,),
        in_specs=[row(sbo.shape[1]), row(lat.shape[1]), row(d), _resident(w["w_uv"].shape),
                  _resident(w["w_out"].shape), _resident((1, d)), _resident((1, d)), _resident(w["w_xq"].shape)],
        out_specs=[row(d), row(d)],
        out_shape=(jax.ShapeDtypeStruct((m, d), F32), jax.ShapeDtypeStruct((m, d), BF16)),
        compiler_params=_params(1),
        name="mixer_out",
    )(sbo, lat, x, w["w_uv"], w["w_out"], w["norm_mix_post"], w["norm_x_pre"], w["w_xq"])


def _norm_matmul_kernel(x_ref, g_ref, w_ref, o_ref, ob_ref, h_ref):
    @pl.when(pl.program_id(1) == 0)
    def _():
        h_ref[...] = _rms(x_ref[...], g_ref[...]).astype(BF16)

    y = _dot(h_ref[...], w_ref[...])
    o_ref[...] = y
    ob_ref[...] = y.astype(BF16)


def _norm_matmul(x, g, w, tm, tn):
    m, d = x.shape
    n = w.shape[1]
    return pl.pallas_call(
        _norm_matmul_kernel,
        grid=(m // tm, n // tn),
        in_specs=[pl.BlockSpec((tm, d), lambda i, j: (i, 0)), pl.BlockSpec((1, d), lambda i, j: (0, 0)),
                  pl.BlockSpec((d, tn), lambda i, j: (0, j))],
        out_specs=[pl.BlockSpec((tm, tn), lambda i, j: (i, j))] * 2,
        out_shape=(jax.ShapeDtypeStruct((m, n), F32), jax.ShapeDtypeStruct((m, n), BF16)),
        scratch_shapes=[pltpu.VMEM((tm, d), BF16)],
        compiler_params=_params(2),
        name="memory_kv",
    )(x, g, w)


def _mem_attn_kernel(q_ref, k_ref, v_ref, o_ref, *, tiled_rows):
    hd_w = q_ref.shape[-1] // MEM_HEADS
    n_lane_tiles = hd_w // LANE
    q = q_ref[...]

    def head_of(ref, hd):
        if not tiled_rows:
            return ref[:, hd * hd_w:(hd + 1) * hd_w].astype(BF16)
        stride = n_lane_tiles * MEM_HEADS
        mt = ref.shape[0] // stride
        return jnp.concatenate([ref[pl.ds(j * MEM_HEADS + hd, mt, stride=stride), :].astype(BF16)
                                for j in range(n_lane_tiles)], axis=1)

    for hd in range(MEM_HEADS):
        sl = slice(hd * hd_w, (hd + 1) * hd_w)
        s = _dot_nt(q[:, sl], head_of(k_ref, hd))
        p = jnp.exp(s - jnp.max(s, axis=-1, keepdims=True))
        o = _dot(p.astype(BF16), head_of(v_ref, hd)) / jnp.sum(p, axis=-1, keepdims=True)
        o_ref[:, sl] = o.astype(o_ref.dtype)


def _mem_attn(q, k, v, tq, tiled_rows=False):
    nb, t, d = q.shape
    kv_block = (None,) + k.shape[1:]
    return pl.pallas_call(
        functools.partial(_mem_attn_kernel, tiled_rows=tiled_rows),
        grid=(nb, t // tq),
        in_specs=[pl.BlockSpec((None, tq, d), lambda b, i: (b, i, 0)),
                  pl.BlockSpec(kv_block, lambda b, i: (b, 0, 0)),
                  pl.BlockSpec(kv_block, lambda b, i: (b, 0, 0))],
        out_specs=pl.BlockSpec((None, tq, d), lambda b, i: (b, i, 0)),
        out_shape=jax.ShapeDtypeStruct((nb, t, d), q.dtype),
        compiler_params=_params(2),
        name="memory_attn",
    )(q, k, v)


def _out_proj_kernel(a_ref, x_ref, w_ref, g_ref, o_ref):
    o_ref[...] = x_ref[...] + _rms(_dot(a_ref[...].astype(BF16), w_ref[...]), g_ref[...])


def _out_proj(a, x, w, g, tm):
    m, d = x.shape
    row = lambda width: pl.BlockSpec((tm, width), lambda i: (i, 0))
    return pl.pallas_call(
        _out_proj_kernel,
        grid=(m // tm,),
        in_specs=[row(a.shape[1]), row(d), _resident(w.shape), _resident((1, d))],
        out_specs=row(d),
        out_shape=jax.ShapeDtypeStruct((m, d), F32),
        compiler_params=_params(1),
        name="cross_out",
    )(a, x, w, g)


def _mlp_kernel(x_ref, gpre_ref, wup_ref, wdown_ref, gpost_ref, o_ref, h_ref, acc_ref):
    f = pl.program_id(1)

    @pl.when(f == 0)
    def _():
        h_ref[...] = _rms(x_ref[...], gpre_ref[...]).astype(BF16)
        acc_ref[...] = jnp.zeros_like(acc_ref)

    u = jnp.maximum(_dot(h_ref[...], wup_ref[...]), 0.0)
    acc_ref[...] += _dot((u * u).astype(BF16), wdown_ref[...])

    @pl.when(f == pl.num_programs(1) - 1)
    def _():
        o_ref[...] = x_ref[...] + _rms(acc_ref[...], gpost_ref[...])


def _mlp(x, gpre, w_up, w_down, gpost, tm, tf):
    m, d = x.shape
    dff = w_up.shape[1]
    return pl.pallas_call(
        _mlp_kernel,
        grid=(m // tm, dff // tf),
        in_specs=[pl.BlockSpec((tm, d), lambda i, f: (i, 0)), pl.BlockSpec((1, d), lambda i, f: (0, 0)),
                  pl.BlockSpec((d, tf), lambda i, f: (0, f)), pl.BlockSpec((tf, d), lambda i, f: (f, 0)),
                  pl.BlockSpec((1, d), lambda i, f: (0, 0))],
        out_specs=pl.BlockSpec((tm, d), lambda i, f: (i, 0)),
        out_shape=jax.ShapeDtypeStruct((m, d), F32),
        scratch_shapes=[pltpu.VMEM((tm, d), BF16), pltpu.VMEM((tm, d), F32)],
        compiler_params=_params(2),
        name="mlp",
    )(x, gpre, w_up, w_down, gpost)


def _prepare_weights(norm_mix_pre, norm_mix_post, w_in, q_a_norm, w_q_up, kv_a_norm, w_kv_up, w_out, norm_mem,
                     norm_x_pre, norm_x_post, w_xq, w_xk, w_xv, w_xo, norm_ffn_pre, norm_ffn_post, w_up, w_down):
    d = w_in.shape[0]
    n_main = SB_Q_W + 2 * SB_KV_W + Q_LORA + KV_LORA
    wq = w_q_up.reshape(Q_LORA, MLA_HEADS, MLA_NOPE + MLA_ROPE)
    wq_nope = wq[:, :, :MLA_NOPE].reshape(Q_LORA, MLA_HEADS * MLA_NOPE)
    wq_rope = jnp.pad(wq[:, :, MLA_NOPE:], ((0, 0), (0, 0), (0, LANE - MLA_ROPE))).reshape(Q_LORA, MLA_HEADS * LANE)
    w_kv = w_kv_up.reshape(KV_LORA, MLA_HEADS, MLA_NOPE + MLA_V)
    vec = lambda g: g.reshape(1, -1).astype(F32)
    return {
        "norm_mix_pre": vec(norm_mix_pre), "norm_mix_post": vec(norm_mix_post), "q_a_norm": vec(q_a_norm),
        "kv_a_norm": vec(kv_a_norm), "norm_mem": vec(norm_mem), "norm_x_pre": vec(norm_x_pre),
        "norm_x_post": vec(norm_x_post), "norm_ffn_pre": vec(norm_ffn_pre), "norm_ffn_post": vec(norm_ffn_post),
        "w_main": w_in[:, :n_main].astype(BF16),
        "w_kpe": jnp.pad(w_in[:, n_main:], ((0, 0), (0, LANE - MLA_ROPE))).astype(BF16),
        "w_q_up": jnp.concatenate([wq_nope, wq_rope], axis=1).astype(BF16),
        "w_uk": jnp.transpose(w_kv[:, :, :MLA_NOPE], (1, 2, 0)).astype(BF16),
        "w_uv": jnp.transpose(w_kv[:, :, MLA_NOPE:], (1, 0, 2)).astype(BF16),
        "w_out": w_out.astype(BF16), "w_xq": w_xq.astype(BF16), "w_xk": w_xk.astype(BF16),
        "w_xv": w_xv.astype(BF16), "w_xo": w_xo.astype(BF16), "w_up": w_up.astype(BF16),
        "w_down": w_down.astype(BF16),
    }


def _row_tile(m, cap):
    t = min(m, cap)
    while m % t:
        t //= 2
    return t


def _tail(x1, attn, w, tm, tf):
    x2 = _out_proj(attn, x1, w["w_xo"], w["norm_x_post"], tm)
    return _mlp(x2, w["norm_ffn_pre"], w["w_up"], w["w_down"], w["norm_ffn_post"], tm, tf)


def kernel(x_prompt, x_sample, mem_prompt, cache_sb_k, cache_sb_v, cache_mla_ckv, cache_mla_kpe, cache_mem_k, cache_mem_v, page_table, norm_mix_pre, norm_mix_post, w_in, q_a_norm, w_q_up, kv_a_norm, w_kv_up, w_out, norm_mem, norm_x_pre, norm_x_post, w_xq, w_xk, w_xv, w_xo, norm_ffn_pre, norm_ffn_post, w_up, w_down):
    depth = w_in.shape[0]
    assert depth == 1, "single-layer step"
    batch, seq, d = x_prompt.shape
    nb, dec, _ = x_sample.shape
    mem_t = mem_prompt.shape[1]
    page = cache_sb_k.shape[2]
    n_pages = page_table.shape[1]
    past_len = n_pages * page
    w = _prepare_weights(*(a[0] for a in (norm_mix_pre, norm_mix_post, w_in, q_a_norm, w_q_up, kv_a_norm, w_kv_up,
                                          w_out, norm_mem, norm_x_pre, norm_x_post, w_xq, w_xk, w_xv, w_xo,
                                          norm_ffn_pre, norm_ffn_post, w_up, w_down)))
    blk = _row_tile(seq, 256)
    tf = _row_tile(w_up.shape[2], 1024)
    pages_step = _row_tile(n_pages, 32)

    ms = nb * dec
    tms = _row_tile(ms, 256)
    xs = x_sample.reshape(ms, d)
    pos_s = past_len + jnp.arange(dec, dtype=jnp.int32)
    pos_tab = jnp.tile(pos_s, tms // dec)
    (sbq, sbk, sbv, sbk_b, sbv_b, ckv, kpe, kcat, qcat) = _proj(xs, pos_tab, w, tms)
    n_pool = cache_sb_k.shape[1]
    sb_rows = lambda cache: cache.reshape(n_pool * page * SB_KV_HEADS, HEAD_DIM)
    sbo = _sb_sample(page_table, sbq.reshape(nb, dec, -1), sbk_b.reshape(nb, dec, -1), sbv_b.reshape(nb, dec, -1),
                     sb_rows(cache_sb_k), sb_rows(cache_sb_v), page, pages_step)
    lat = _mla_sample(page_table, qcat.reshape(nb, dec, -1), kcat.reshape(nb, dec, -1),
                      cache_mla_ckv.reshape(n_pool, page, KV_LORA),
                      jnp.swapaxes(cache_mla_kpe.reshape(n_pool, page, MLA_ROPE), 1, 2), pages_step)
    x1, qm = _mix_out(sbo.reshape(ms, -1), lat.reshape(ms, -1), xs, w, tms)
    n_lane_tiles = d // MEM_HEADS // LANE
    mem_rows = lambda cache: jnp.swapaxes(cache.reshape(nb, mem_t, MEM_HEADS, n_lane_tiles, LANE), 2, 3).reshape(
        nb, mem_t * MEM_HEADS * n_lane_tiles, LANE)
    attn = _mem_attn(qm.reshape(nb, dec, d), mem_rows(cache_mem_k), mem_rows(cache_mem_v), dec, tiled_rows=True)
    y_sample = _tail(x1, attn.reshape(ms, d), w, _row_tile(ms, 512), tf).reshape(nb, dec, d)
    sample_caches = (sbk.reshape(1, nb, dec, SB_KV_HEADS, HEAD_DIM), sbv.reshape(1, nb, dec, SB_KV_HEADS, HEAD_DIM),
                     ckv.reshape(1, nb, dec, KV_LORA), kpe.reshape(1, nb, dec, MLA_ROPE))

    mp = batch * seq
    tm = _row_tile(mp, 256)
    xp = x_prompt.reshape(mp, d)
    (sbq, sbk, sbv, sbk_b, sbv_b, ckv, kpe, kcat, qcat) = _proj(xp, jnp.arange(seq, dtype=jnp.int32), w, tm)
    sbo = _sb_prompt(sbq, sbk_b, sbv_b, batch, seq, blk)
    lat = _mla_prompt(qcat, kcat, batch, seq, _row_tile(seq, 128), _row_tile(seq, 512))
    x1, qm = _mix_out(sbo, lat, xp, w, tm)
    mem = mem_prompt.reshape(batch * mem_t, d)
    tmm = _row_tile(batch * mem_t, 512)
    mk, mk_b = _norm_matmul(mem, w["norm_mem"], w["w_xk"], tmm, 1024)
    mv, mv_b = _norm_matmul(mem, w["norm_mem"], w["w_xv"], tmm, 1024)
    attn = _mem_attn(qm.reshape(batch, seq, d), mk_b.reshape(batch, mem_t, d), mv_b.reshape(batch, mem_t, d), blk)
    y_prompt = _tail(x1, attn.reshape(mp, d), w, _row_tile(mp, 512), tf).reshape(batch, seq, d)
    prompt_caches = (sbk.reshape(1, batch, seq, SB_KV_HEADS, HEAD_DIM), sbv.reshape(1, batch, seq, SB_KV_HEADS, HEAD_DIM),
                     ckv.reshape(1, batch, seq, KV_LORA), kpe.reshape(1, batch, seq, MLA_ROPE),
                     mk.reshape(1, batch, mem_t, MEM_HEADS, d // MEM_HEADS),
                     mv.reshape(1, batch, mem_t, MEM_HEADS, d // MEM_HEADS))

    return (y_prompt, y_sample) + prompt_caches + sample_caches
```
